```python
import jax, jax.numpy as jnp
from jax import lax
import numpy as np

D_MODEL = 1024
BATCH = 4
SEQ = 4096
DEPTH = 4
DEC_BATCH = 128
DEC_SEQ = 8
PAST_LEN = 8192
PAGE_SIZE = 128

D_CONV = 512
CONV_K = 31
MLA_HEADS = 8
MLA_NOPE = 64
MLA_ROPE = 32
MLA_V = 64
Q_RANK = 384
KV_RANK = 256
ROPE_THETA = 10000.0
MLA_SCALE = (MLA_NOPE + MLA_ROPE) ** -0.5
SB_HEADS = 8
SB_KV_HEADS = 2
SB_DIM = 64
SB_SCALE = SB_DIM ** -0.5
D_FF = 2816
FFN_K = 3
N_BRANCH = 3
Q_BLOCK = 128
EPS = 1e-6
_IN_SIZES = (D_CONV, D_CONV, Q_RANK, KV_RANK, MLA_ROPE, SB_HEADS * SB_DIM,
             SB_KV_HEADS * SB_DIM, SB_KV_HEADS * SB_DIM, N_BRANCH * D_MODEL)
D_IN = sum(_IN_SIZES)

kernel_name = 'hybrid_conv_mla_stickbreak_decoder_step'


def rmsnorm(x, g):
    xf = x.astype(jnp.float32)
    y = xf * lax.rsqrt(jnp.mean(xf * xf, axis=-1, keepdims=True) + EPS)
    return (y * g.astype(jnp.float32)).astype(x.dtype)


def layernorm(x, g, b):
    xf = x.astype(jnp.float32)
    xc = xf - jnp.mean(xf, axis=-1, keepdims=True)
    y = xc * lax.rsqrt(jnp.mean(xc * xc, axis=-1, keepdims=True) + EPS)
    return (y * g.astype(jnp.float32) + b.astype(jnp.float32)).astype(x.dtype)


def rope(x, pos):
    half = x.shape[-1] // 2
    inv_freq = ROPE_THETA ** (-jnp.arange(half, dtype=jnp.float32) / half)
    ang = pos.astype(jnp.float32)[:, None] * inv_freq[None, :]
    shape = (ang.shape[0],) + (1,) * (x.ndim - 3) + (half,)
    cos = jnp.cos(ang).reshape(shape)
    sin = jnp.sin(ang).reshape(shape)
    xf = x.astype(jnp.float32)
    x1, x2 = xf[..., :half], xf[..., half:]
    return jnp.concatenate([x1 * cos - x2 * sin, x1 * sin + x2 * cos], axis=-1).astype(x.dtype)


def causal_dwconv(u, ctx, w, b):
    full = jnp.concatenate([ctx.astype(u.dtype), u], axis=1)
    y = lax.conv_general_dilated(full, w.astype(u.dtype)[:, None, :], window_strides=(1,), padding='VALID',
                                 dimension_numbers=('NWC', 'WIO', 'NWC'), feature_group_count=u.shape[-1])
    return y + b.astype(u.dtype), full[:, full.shape[1] - (w.shape[0] - 1):]


def split_in_proj(z):
    cuts, acc = [], 0
    for s in _IN_SIZES[:-1]:
        acc += s
        cuts.append(acc)
    return jnp.split(z, cuts, axis=-1)


def over_query_blocks(attend, q_args, q_pos):
    b, t = q_args[0].shape[:2]
    nb = t // Q_BLOCK
    split = lambda a: jnp.moveaxis(a.reshape((b, nb, Q_BLOCK) + a.shape[2:]), 1, 0)
    xs = tuple(split(a) for a in q_args) + (q_pos.reshape(nb, Q_BLOCK),)
    out = lax.map(lambda blk: attend(*blk), xs)
    out = jnp.moveaxis(out, 0, 1)
    return out.reshape((b, t) + out.shape[3:])


def mla_attend(q_lat, q_pe, q_pos, segs):
    scores = []
    for ckv, kpe, k_pos in segs:
        s = (jnp.einsum('bqhr,bkr->bhqk', q_lat, ckv) + jnp.einsum('bqhp,bkp->bhqk', q_pe, kpe)).astype(jnp.float32) * MLA_SCALE
        scores.append(jnp.where(k_pos[None, :] <= q_pos[:, None], s, -jnp.inf))
    prob = jax.nn.softmax(jnp.concatenate(scores, axis=-1), axis=-1)
    outs, off = [], 0
    for ckv, _, _ in segs:
        n = ckv.shape[1]
        outs.append(jnp.einsum('bhqk,bkr->bqhr', prob[..., off:off + n].astype(ckv.dtype), ckv))
        off += n
    return sum(outs[1:], outs[0])


def sb_attend(q, q_pos, segs):
    z = jnp.concatenate([jnp.einsum('bqgnd,bkgd->bgnqk', q, k).astype(jnp.float32) * SB_SCALE for k, _, _ in segs], axis=-1)
    mask = jnp.concatenate([k_pos[None, :] < q_pos[:, None] for _, _, k_pos in segs], axis=-1)
    log_stay = jnp.where(mask, jax.nn.log_sigmoid(-z), 0.0)
    log_after = lax.cumsum(log_stay, axis=4, reverse=True) - log_stay
    a = jnp.where(mask, jnp.exp(jax.nn.log_sigmoid(z) + log_after), 0.0)
    outs, off = [], 0
    for _, v, _ in segs:
        n = v.shape[1]
        outs.append(jnp.einsum('bgnqk,bkgd->bqgnd', a[..., off:off + n].astype(v.dtype), v))
        off += n
    return sum(outs[1:], outs[0])


def token_mixers(h, pos, past, p):
    b, t, _ = h.shape
    u_a, u_b, c_q, c_kv, k_pe, sb_q, sb_k, sb_v, gate_in = split_in_proj(h @ p['w_in'])
    u = u_a * jax.nn.sigmoid(u_b)
    ctx = jnp.zeros((b, CONV_K - 1, D_CONV), u.dtype) if past is None else past['conv']
    u, conv_state = causal_dwconv(u, ctx, p['w_dw'], p['b_dw'])
    br_a = jax.nn.silu(layernorm(u, p['g_conv_ln'], p['b_conv_ln'])) @ p['w_conv_out']
    q = (rmsnorm(c_q, p['g_q']) @ p['w_uq']).reshape(b, t, MLA_HEADS, MLA_NOPE + MLA_ROPE)
    q_lat = jnp.einsum('bthn,rhn->bthr', q[..., :MLA_NOPE], p['w_uk'])
    q_pe = rope(q[..., MLA_NOPE:], pos)
    c_kv = rmsnorm(c_kv, p['g_kv'])
    k_pe = rope(k_pe, pos)
    mla_segs = [(c_kv, k_pe, pos)]
    sb_q = sb_q.reshape(b, t, SB_KV_HEADS, SB_HEADS // SB_KV_HEADS, SB_DIM)
    sb_k = sb_k.reshape(b, t, SB_KV_HEADS, SB_DIM)
    sb_v = sb_v.reshape(b, t, SB_KV_HEADS, SB_DIM)
    sb_segs = [(sb_k, sb_v, pos)]
    if past is None:
        o_lat = over_query_blocks(lambda ql, qp, qpos: mla_attend(ql, qp, qpos, mla_segs), (q_lat, q_pe), pos)
        o_sb = over_query_blocks(lambda qq, qpos: sb_attend(qq, qpos, sb_segs), (sb_q,), pos)
    else:
        past_pos = jnp.arange(past['ckv'].shape[1], dtype=jnp.int32)
        o_lat = mla_attend(q_lat, q_pe, pos, [(past['ckv'], past['kpe'], past_pos)] + mla_segs)
        o_sb = sb_attend(sb_q, pos, [(past['sbk'], past['sbv'], past_pos)] + sb_segs)
    o_mla = jnp.einsum('bthr,rhv->bthv', o_lat, p['w_uv']).reshape(b, t, MLA_HEADS * MLA_V)
    br_b = o_mla @ p['w_mla_out']
    br_c = o_sb.reshape(b, t, SB_HEADS * SB_DIM) @ p['w_sb_out']
    g = jax.nn.sigmoid(gate_in + p['b_gate']).reshape(b, t, N_BRANCH, D_MODEL)
    merged = g[:, :, 0] * br_a + g[:, :, 1] * br_b + g[:, :, 2] * br_c
    return merged @ p['w_out'], (c_kv, k_pe, sb_k, sb_v, conv_state)


def conv_ffn(h, ctx, p):
    up, new_ctx = causal_dwconv(h @ p['w_ffn_up'], ctx, p['w_ffn_dw'], p['b_ffn_dw'])
    a, gl = jnp.split(up, 2, axis=-1)
    return (jax.nn.silu(a) * gl) @ p['w_ffn_down'], new_ctx


def block(x, c, pos, past, p):
    b = x.shape[0]
    mod = jax.nn.silu(c) @ p['w_ada'] + p['b_ada']
    sh1, sc1, gt1, sh2, sc2, gt2 = jnp.split(mod[:, None, :], 6, axis=-1)
    h = rmsnorm(x, p['g_norm_mix']) * (1 + sc1) + sh1
    mix, st = token_mixers(h, pos, past, p)
    x = x + gt1 * mix
    h = rmsnorm(x, p['g_norm_ffn']) * (1 + sc2) + sh2
    ctx = jnp.zeros((b, FFN_K - 1, 2 * D_FF), h.dtype) if past is None else past['ffn']
    f, ffn_state = conv_ffn(h, ctx, p)
    x = x + gt2 * f
    return x, st + (ffn_state,)


def gather_pages(cache, layer, page_table):
    g = cache[layer, page_table]
    return g.reshape((page_table.shape[0], g.shape[1] * g.shape[2]) + g.shape[3:])


def setup_inputs(seed: int = 0) -> dict:
    key = jax.random.key(seed)
    ks = iter(jax.random.split(key, 48))
    nrm = lambda shape, s: s * jax.random.normal(next(ks), shape, jnp.float32)
    gain = lambda shape: 1.0 + 0.05 * jax.random.normal(next(ks), shape, jnp.float32)
    n_pages = PAST_LEN // PAGE_SIZE
    n_pool = (DEC_BATCH * n_pages * 5) // 4
    perm = jax.random.permutation(next(ks), n_pool)
    page_table = perm[:DEC_BATCH * n_pages].reshape(DEC_BATCH, n_pages).astype(jnp.int32)
    L = DEPTH
    return {
        'x_prompt': nrm((BATCH, SEQ, D_MODEL), 1.0),
        'x_sample': nrm((DEC_BATCH, DEC_SEQ, D_MODEL), 1.0),
        'c_prompt': nrm((BATCH, D_MODEL), 1.0),
        'c_sample': nrm((DEC_BATCH, D_MODEL), 1.0),
        'cache_mla_latent': nrm((L, n_pool, PAGE_SIZE, KV_RANK), 1.0),
        'cache_mla_krope': nrm((L, n_pool, PAGE_SIZE, MLA_ROPE), 1.0),
        'cache_sb_k': nrm((L, n_pool, PAGE_SIZE, SB_KV_HEADS, SB_DIM), 1.0),
        'cache_sb_v': nrm((L, n_pool, PAGE_SIZE, SB_KV_HEADS, SB_DIM), 1.0),
        'state_conv': nrm((L, DEC_BATCH, CONV_K - 1, D_CONV), 0.5),
        'state_ffn_conv': nrm((L, DEC_BATCH, FFN_K - 1, 2 * D_FF), 1.0),
        'page_table': page_table,
        'w_ada': nrm((L, D_MODEL, 6 * D_MODEL), 0.5 * D_MODEL ** -0.5),
        'b_ada': nrm((L, 6 * D_MODEL), 0.02),
        'g_norm_mix': gain((L, D_MODEL)),
        'g_norm_ffn': gain((L, D_MODEL)),
        'w_in': nrm((L, D_MODEL, D_IN), D_MODEL ** -0.5),
        'b_gate': nrm((L, N_BRANCH * D_MODEL), 0.02),
        'w_dw': nrm((L, CONV_K, D_CONV), CONV_K ** -0.5),
        'b_dw': nrm((L, D_CONV), 0.02),
        'g_conv_ln': gain((L, D_CONV)),
        'b_conv_ln': nrm((L, D_CONV), 0.02),
        'w_conv_out': nrm((L, D_CONV, D_MODEL), D_CONV ** -0.5),
        'g_q': gain((L, Q_RANK)),
        'w_uq': nrm((L, Q_RANK, MLA_HEADS * (MLA_NOPE + MLA_ROPE)), Q_RANK ** -0.5),
        'g_kv': gain((L, KV_RANK)),
        'w_uk': nrm((L, KV_RANK, MLA_HEADS, MLA_NOPE), KV_RANK ** -0.5),
        'w_uv': nrm((L, KV_RANK, MLA_HEADS, MLA_V), KV_RANK ** -0.5),
        'w_mla_out': nrm((L, MLA_HEADS * MLA_V, D_MODEL), (MLA_HEADS * MLA_V) ** -0.5),
        'w_sb_out': nrm((L, SB_HEADS * SB_DIM, D_MODEL), (SB_HEADS * SB_DIM) ** -0.5),
        'w_out': nrm((L, D_MODEL, D_MODEL), D_MODEL ** -0.5),
        'w_ffn_up': nrm((L, D_MODEL, 2 * D_FF), D_MODEL ** -0.5),
        'w_ffn_dw': nrm((L, FFN_K, 2 * D_FF), FFN_K ** -0.5),
        'b_ffn_dw': nrm((L, 2 * D_FF), 0.02),
        'w_ffn_down': nrm((L, D_FF, D_MODEL), D_FF ** -0.5),
        'g_final': gain((D_MODEL,)),
    }


def reference(x_prompt, x_sample, c_prompt, c_sample, cache_mla_latent, cache_mla_krope, cache_sb_k, cache_sb_v,
              state_conv, state_ffn_conv, page_table, w_ada, b_ada, g_norm_mix, g_norm_ffn, w_in, b_gate,
              w_dw, b_dw, g_conv_ln, b_conv_ln, w_conv_out, g_q, w_uq, g_kv, w_uk, w_uv, w_mla_out, w_sb_out,
              w_out, w_ffn_up, w_ffn_dw, b_ffn_dw, w_ffn_down, g_final):
    pos_p = jnp.arange(x_prompt.shape[1], dtype=jnp.int32)
    pos_s = PAST_LEN + jnp.arange(x_sample.shape[1], dtype=jnp.int32)
    yp, ys = x_prompt, x_sample
    st_p, st_s = [], []
    for l in range(DEPTH):
        p = dict(w_ada=w_ada[l], b_ada=b_ada[l], g_norm_mix=g_norm_mix[l], g_norm_ffn=g_norm_ffn[l],
                 w_in=w_in[l], b_gate=b_gate[l], w_dw=w_dw[l], b_dw=b_dw[l], g_conv_ln=g_conv_ln[l],
                 b_conv_ln=b_conv_ln[l], w_conv_out=w_conv_out[l], g_q=g_q[l], w_uq=w_uq[l], g_kv=g_kv[l],
                 w_uk=w_uk[l], w_uv=w_uv[l], w_mla_out=w_mla_out[l], w_sb_out=w_sb_out[l], w_out=w_out[l],
                 w_ffn_up=w_ffn_up[l], w_ffn_dw=w_ffn_dw[l], b_ffn_dw=b_ffn_dw[l], w_ffn_down=w_ffn_down[l])
        yp, sp = block(yp, c_prompt, pos_p, None, p)
        past = dict(ckv=gather_pages(cache_mla_latent, l, page_table),
                    kpe=gather_pages(cache_mla_krope, l, page_table),
                    sbk=gather_pages(cache_sb_k, l, page_table),
                    sbv=gather_pages(cache_sb_v, l, page_table),
                    conv=state_conv[l], ffn=state_ffn_conv[l])
        ys, ss = block(ys, c_sample, pos_s, past, p)
        st_p.append(sp)
        st_s.append(ss)
    y_prompt = rmsnorm(yp, g_final)
    y_sample = rmsnorm(ys, g_final)
    stack = lambda sts, i: jnp.stack([s[i] for s in sts], axis=0)
    return (y_prompt, y_sample,
            stack(st_p, 0), stack(st_p, 1), stack(st_p, 2), stack(st_p, 3), stack(st_p, 4), stack(st_p, 5),
            stack(st_s, 0), stack(st_s, 1), stack(st_s, 2), stack(st_s, 3), stack(st_s, 4), stack(st_s, 5))
```

```python
import functools

import jax
import jax.numpy as jnp
import numpy as np
from jax import lax
from jax.experimental import pallas as pl
from jax.experimental.pallas import tpu as pltpu

F32 = jnp.float32
BF = jnp.bfloat16

D_MODEL = 1024
D_CONV = 512
CONV_K = 31
MLA_HEADS = 8
MLA_NOPE = 64
MLA_ROPE = 32
MLA_V = 64
Q_RANK = 384
KV_RANK = 256
ROPE_THETA = 10000.0
MLA_SCALE = (MLA_NOPE + MLA_ROPE) ** -0.5
SB_HEADS = 8
SB_KV_HEADS = 2
SB_DIM = 64
SB_SCALE = SB_DIM ** -0.5
D_FF = 2816
FFN_K = 3
N_BRANCH = 3
EPS = 1e-6
IN_SIZES = (D_CONV, D_CONV, Q_RANK, KV_RANK, MLA_ROPE, SB_HEADS * SB_DIM,
            SB_KV_HEADS * SB_DIM, SB_KV_HEADS * SB_DIM, N_BRANCH * D_MODEL)

LANE = 128
SUBLANE = 8
VMEM_LIMIT = 56 * 1024 * 1024

HEAD_PAD = LANE
C_UA = 0
C_UB = C_UA + D_CONV
C_CQ = C_UB + D_CONV
C_CKV = C_CQ + Q_RANK
C_KPE = C_CKV + KV_RANK
C_KPER = C_KPE + LANE
C_SBQ = C_KPER + LANE
C_SBK = C_SBQ + SB_HEADS * HEAD_PAD
C_SBV = C_SBK + SB_KV_HEADS * SB_DIM
C_GATE = C_SBV + SB_KV_HEADS * SB_DIM
C_END = C_GATE + N_BRANCH * D_MODEL
GATE_CHUNK = 512

ROW_TILE = 256
ATT_BLOCK = 512
SB_BLOCK = 256
CONV_ROWS = 64
FFN_CHUNK = 256
PAGES_PER_CHUNK = 16
SEQS_PER_TILE = ROW_TILE // SUBLANE


def _params(*sem):
    return pltpu.CompilerParams(dimension_semantics=sem, vmem_limit_bytes=VMEM_LIMIT)


def _const_spec(shape):
    nd = len(shape)
    return pl.BlockSpec(shape, lambda *_: (0,) * nd)


def _rms(v, g):
    return v * lax.rsqrt(jnp.mean(v * v, axis=-1, keepdims=True) + EPS) * g


def _softplus(z):
    return jnp.maximum(z, 0.0) + jnp.log1p(jnp.exp(-jnp.abs(z)))


def _dot(a, b):
    return jnp.dot(a, b, preferred_element_type=F32)


def _dot_nt(a, b):
    return lax.dot_general(a, b, (((1,), (1,)), ((), ())), preferred_element_type=F32)


def _lane_tile(v, n):
    return v if n == 1 else jnp.concatenate([v] * n, axis=-1)


def _ada_kernel(c_ref, w_ref, b_ref, o_ref):
    c = c_ref[...]
    a = (c * jax.nn.sigmoid(c)).astype(BF)
    o_ref[0] = _dot(a, w_ref[0].astype(BF)) + b_ref[0]


def _ada(c_all, w_ada, b_ada):
    depth, d, n = w_ada.shape
    rows = c_all.shape[0]
    tn = 512
    return pl.pallas_call(
        _ada_kernel,
        grid=(depth, n // tn),
        in_specs=[pl.BlockSpec((rows, d), lambda l, j: (0, 0)),
                  pl.BlockSpec((1, d, tn), lambda l, j: (l, 0, j)),
                  pl.BlockSpec((1, 1, tn), lambda l, j: (l, 0, j))],
        out_specs=pl.BlockSpec((1, rows, tn), lambda l, j: (l, 0, j)),
        out_shape=jax.ShapeDtypeStruct((depth, rows, n), F32),
        compiler_params=_params("arbitrary", "arbitrary"),
        name="ada",
    )(c_all, w_ada, b_ada.reshape(depth, 1, n))


def _inproj_kernel(*refs, prompt):
    (x_ref, mod_ref, tab_ref, gn_ref, wmain_ref, gq_ref, wq_ref, wqr_ref, gkv_ref, bgate_ref) = refs[:10]
    if prompt:
        wk_ref, wv_ref = refs[10:12]
        (u_ref, q_ref, lat_ref, kpe_ref, sbq_ref, sbk_ref, sbv_ref, gate_ref,
         k_ref, v_ref, sbkb_ref, sbvb_ref) = refs[12:]
    else:
        wukt_ref, wqpe_ref, wqper_ref = refs[10:13]
        (u_ref, lat_ref, kpe_ref, sbq_ref, sbk_ref, sbv_ref, gate_ref,
         qlat_ref, qpe_ref) = refs[13:]
    ts, tr, d = x_ref.shape
    tm = ts * tr

    def rows3(v):
        return v.reshape(ts, tr, v.shape[-1])

    mod = mod_ref[...]
    h = _rms(x_ref[...], gn_ref[...]) * (1.0 + mod[:, 1:2, :]) + mod[:, 0:1, :]
    hb = h.reshape(tm, d).astype(BF)

    def seg(a, b):
        return _dot(hb, wmain_ref[:, a:b])

    tab = tab_ref[...]

    u_ref[...] = rows3(seg(C_UA, C_UB) * jax.nn.sigmoid(seg(C_UB, C_CQ)))

    cqn = _rms(seg(C_CQ, C_CKV), gq_ref[...]).astype(BF)
    qf = (rows3(_dot(cqn, wq_ref[...])) * _lane_tile(tab[0], MLA_HEADS)[None]
          + rows3(_dot(cqn, wqr_ref[...])) * _lane_tile(tab[1], MLA_HEADS)[None])
    if prompt:
        q_ref[...] = qf.astype(BF)

    lat = _rms(seg(C_CKV, C_KPE), gkv_ref[...])
    lat_ref[...] = rows3(lat)
    kr = rows3(seg(C_KPE, C_KPER)) * tab[2][None] + rows3(seg(C_KPER, C_SBQ)) * tab[3][None]
    kpe_ref[...] = kr[:, :, :MLA_ROPE]

    if prompt:
        latb = lat.astype(BF)
        lane = lax.broadcasted_iota(jnp.int32, (1, 1, LANE), 2)
        kr_head = jnp.where((lane >= MLA_NOPE) & (lane < MLA_NOPE + MLA_ROPE), kr, 0.0)
        k_ref[...] = (rows3(_dot(latb, wk_ref[...])) + _lane_tile(kr_head, MLA_HEADS)).astype(BF)
        v_ref[...] = rows3(_dot(latb, wv_ref[...])).astype(BF)
    else:
        qfb = qf.reshape(tm, MLA_HEADS * HEAD_PAD).astype(BF)
        for hd in range(MLA_HEADS):
            ql = _dot(qfb[:, hd * HEAD_PAD:(hd + 1) * HEAD_PAD], wukt_ref[hd])
            qlat_ref[:, :, hd * KV_RANK:(hd + 1) * KV_RANK] = rows3(ql)
        qpe_ref[...] = (rows3(_dot(cqn, wqpe_ref[...])) * _lane_tile(tab[4], MLA_HEADS)[None]
                        + rows3(_dot(cqn, wqper_ref[...])) * _lane_tile(tab[5], MLA_HEADS)[None])

    sbq_ref[...] = rows3(seg(C_SBQ, C_SBK) * SB_SCALE).astype(sbq_ref.dtype)
    sbk = rows3(seg(C_SBK, C_SBV))
    sbv = rows3(seg(C_SBV, C_GATE))
    sbk_ref[...] = sbk
    sbv_ref[...] = sbv
    if prompt:
        sbkb_ref[...] = sbk.astype(BF)
        sbvb_ref[...] = sbv.astype(BF)

    for c in range(0, N_BRANCH * D_MODEL, GATE_CHUNK):
        g = jax.nn.sigmoid(seg(C_GATE + c, C_GATE + c + GATE_CHUNK) + bgate_ref[:, c:c + GATE_CHUNK])
        gate_ref[:, :, c:c + GATE_CHUNK] = rows3(g).astype(BF)


def _inproj(x, mod, tab, wl, *, prompt, ts, tr):
    s, r, d = x.shape
    grid = (s // ts, r // tr)

    def row_spec(c):
        return pl.BlockSpec((ts, tr, c), lambda i, j: (i, j, 0))

    def row_shape(c, dt):
        return jax.ShapeDtypeStruct((s, r, c), dt)

    weights = [wl["g_norm_mix"], wl["w_main"], wl["g_q"], wl["w_q"], wl["w_qr"], wl["g_kv"], wl["b_gate"]]
    if prompt:
        weights += [wl["w_k"], wl["w_v"]]
    else:
        weights += [wl["w_ukt"], wl["w_qpe"], wl["w_qper"]]
    in_specs = [row_spec(d),
                pl.BlockSpec((ts, 6, d), lambda i, j: (i, 0, 0)),
                pl.BlockSpec((6, tr, LANE), lambda i, j: (0, j, 0))]
    in_specs += [_const_spec(w.shape) for w in weights]
    outs = [(D_CONV, F32)] + ([(MLA_HEADS * HEAD_PAD, BF)] if prompt else [])
    outs += [(KV_RANK, F32), (MLA_ROPE, F32), (SB_HEADS * HEAD_PAD, BF if prompt else F32),
             (SB_KV_HEADS * SB_DIM, F32), (SB_KV_HEADS * SB_DIM, F32), (N_BRANCH * D_MODEL, BF)]
    if prompt:
        outs += [(MLA_HEADS * HEAD_PAD, BF), (MLA_HEADS * MLA_V, BF),
                 (SB_KV_HEADS * SB_DIM, BF), (SB_KV_HEADS * SB_DIM, BF)]
    else:
        outs += [(MLA_HEADS * KV_RANK, F32), (MLA_HEADS * HEAD_PAD, F32)]
    return pl.pallas_call(
        functools.partial(_inproj_kernel, prompt=prompt),
        grid=grid,
        in_specs=in_specs,
        out_specs=[row_spec(c) for c, _ in outs],
        out_shape=[row_shape(c, dt) for c, dt in outs],
        compiler_params=_params("arbitrary", "arbitrary"),
        name="inproj_prompt" if prompt else "inproj_sample",
    )(x, mod, tab, *weights)


def _ln_swish(acc, g, b):
    xc = acc - jnp.mean(acc, axis=-1, keepdims=True)
    y = xc * lax.rsqrt(jnp.mean(xc * xc, axis=-1, keepdims=True) + EPS) * g + b
    return y * jax.nn.sigmoid(y)


def _conv_prompt_kernel(cur_ref, prev_ref, w_ref, b_ref, g_ref, bl_ref, o_ref, win_ref):
    tm = cur_ref.shape[1]
    halo = prev_ref.shape[1]
    j = pl.program_id(1)
    win_ref[0:halo, :] = jnp.where(j > 0, prev_ref[0], 0.0)
    win_ref[halo:, :] = cur_ref[0]
    off = halo - (CONV_K - 1)
    for r0 in range(0, tm, CONV_ROWS):
        acc = jnp.broadcast_to(b_ref[...], (CONV_ROWS, D_CONV))
        for k in range(CONV_K):
            acc = acc + w_ref[k:k + 1, :] * win_ref[off + r0 + k:off + r0 + k + CONV_ROWS, :]
        o_ref[0, r0:r0 + CONV_ROWS, :] = _ln_swish(acc, g_ref[...], bl_ref[...]).astype(BF)


def _conv_prompt(u, wl, *, tm):
    b, t, c = u.shape
    halo = 32
    ratio = tm // halo
    return pl.pallas_call(
        _conv_prompt_kernel,
        grid=(b, t // tm),
        in_specs=[pl.BlockSpec((1, tm, c), lambda i, j: (i, j, 0)),
                  pl.BlockSpec((1, halo, c), lambda i, j: (i, jnp.maximum(j * ratio - 1, 0), 0)),
                  _const_spec(wl["w_dw"].shape), _const_spec(wl["b_dw"].shape),
                  _const_spec(wl["g_conv_ln"].shape), _const_spec(wl["b_conv_ln"].shape)],
        out_specs=pl.BlockSpec((1, tm, c), lambda i, j: (i, j, 0)),
        out_shape=jax.ShapeDtypeStruct((b, t, c), BF),
        scratch_shapes=[pltpu.VMEM((tm + halo, c), F32)],
        compiler_params=_params("arbitrary", "arbitrary"),
        name="conv_prompt",
    )(u, u, wl["w_dw"], wl["b_dw"], wl["g_conv_ln"], wl["b_conv_ln"])


def _conv_sample_kernel(full_ref, w_ref, b_ref, g_ref, bl_ref, o_ref):
    ts, _, c = full_ref.shape
    tr = o_ref.shape[1]
    acc = jnp.broadcast_to(b_ref[...][None], (ts, tr, c))
    for k in range(CONV_K):
        acc = acc + w_ref[k:k + 1, :][None] * full_ref[:, k:k + tr, :]
    o_ref[...] = _ln_swish(acc, g_ref[...][None], bl_ref[...][None]).astype(BF)


def _conv_sample(full, wl, *, ts, tr):
    s, rows, c = full.shape
    return pl.pallas_call(
        _conv_sample_kernel,
        grid=(s // ts,),
        in_specs=[pl.BlockSpec((ts, rows, c), lambda i: (i, 0, 0)),
                  _const_spec(wl["w_dw"].shape), _const_spec(wl["b_dw"].shape),
                  _const_spec(wl["g_conv_ln"].shape), _const_spec(wl["b_conv_ln"].shape)],
        out_specs=pl.BlockSpec((ts, tr, c), lambda i: (i, 0, 0)),
        out_shape=jax.ShapeDtypeStruct((s, tr, c), BF),
        compiler_params=_params("arbitrary"),
        name="conv_sample",
    )(full, wl["w_dw"], wl["b_dw"], wl["g_conv_ln"], wl["b_conv_ln"])


def _mla_prompt_kernel(q_ref, k_ref, v_ref, o_ref, m_ref, l_ref, acc_ref):
    qi = pl.program_id(1)
    kb = pl.program_id(2)
    tq = q_ref.shape[1]
    tk = k_ref.shape[1]

    @pl.when(kb == 0)
    def _():
        m_ref[...] = jnp.full(m_ref.shape, -1e30, F32)
        l_ref[...] = jnp.zeros(l_ref.shape, F32)
        acc_ref[...] = jnp.zeros(acc_ref.shape, F32)

    @pl.when(kb <= qi)
    def _():
        row = qi * tq + lax.broadcasted_iota(jnp.int32, (tq, tk), 0)
        col = kb * tk + lax.broadcasted_iota(jnp.int32, (tq, tk), 1)
        visible = col <= row
        for hd in range(MLA_HEADS):
            s = _dot_nt(q_ref[0, :, hd * HEAD_PAD:(hd + 1) * HEAD_PAD],
                        k_ref[0, :, hd * HEAD_PAD:(hd + 1) * HEAD_PAD])
            s = jnp.where(visible, s, -1e30)
            m_prev = m_ref[hd]
            m_new = jnp.maximum(m_prev, jnp.max(s, axis=1, keepdims=True))
            alpha = jnp.exp(m_prev - m_new)
            p = jnp.exp(s - _lane_tile(m_new, tk // LANE))
            l_ref[hd] = alpha * l_ref[hd] + jnp.sum(p, axis=1, keepdims=True)
            m_ref[hd] = m_new
            pair = hd // 2
            acc_ref[hd] = alpha * acc_ref[hd] + _dot(p.astype(BF), v_ref[0, :, pair * LANE:(pair + 1) * LANE])

    @pl.when(kb == qi)
    def _():
        lane = lax.broadcasted_iota(jnp.int32, (tq, LANE), 1)
        for pair in range(MLA_HEADS // 2):
            even = acc_ref[2 * pair] / l_ref[2 * pair]
            odd = acc_ref[2 * pair + 1] / l_ref[2 * pair + 1]
            o_ref[0, :, pair * LANE:(pair + 1) * LANE] = jnp.where(lane < MLA_V, even, odd).astype(BF)


def _mla_prompt(q, k, v, *, blk):
    b, t, _ = q.shape
    n = t // blk
    return pl.pallas_call(
        _mla_prompt_kernel,
        grid=(b, n, n),
        in_specs=[pl.BlockSpec((1, blk, q.shape[2]), lambda i, qi, kb: (i, qi, 0)),
                  pl.BlockSpec((1, blk, k.shape[2]), lambda i, qi, kb: (i, jnp.minimum(kb, qi), 0)),
                  pl.BlockSpec((1, blk, v.shape[2]), lambda i, qi, kb: (i, jnp.minimum(kb, qi), 0))],
        out_specs=pl.BlockSpec((1, blk, v.shape[2]), lambda i, qi, kb: (i, qi, 0)),
        out_shape=jax.ShapeDtypeStruct((b, t, v.shape[2]), BF),
        scratch_shapes=[pltpu.VMEM((MLA_HEADS, blk, LANE), F32)] * 3,
        compiler_params=_params("arbitrary", "arbitrary", "arbitrary"),
        name="mla_prompt",
    )(q, k, v)


def _suffix_matrix(n):
    s = np.arange(n)[:, None]
    j = np.arange(n)[None, :]
    return jnp.asarray((s > j).astype(np.float32), dtype=BF)


def _sb_block(z, visible, u, carry, v, v_transposed=False):
    tk = z.shape[1]
    sp = _softplus(z)
    log_stay = -sp if visible is None else jnp.where(visible, -sp, 0.0)
    hi = log_stay.astype(BF)
    lo = (log_stay - hi.astype(F32)).astype(BF)
    log_after = _dot(hi, u) + _dot(lo, u) + _lane_tile(carry, tk // LANE)
    a = jnp.exp(z - sp + log_after)
    if visible is not None:
        a = jnp.where(visible, a, 0.0)
    pv = _dot_nt(a.astype(BF), v) if v_transposed else _dot(a.astype(BF), v)
    return pv, carry + jnp.sum(log_stay, axis=1, keepdims=True)


def _sb_pair_out(even, odd, group):
    lane = lax.broadcasted_iota(jnp.int32, even.shape, 1)
    if group == 0:
        return jnp.where(lane < SB_DIM, even, pltpu.roll(odd, SB_DIM, 1))
    return jnp.where(lane < SB_DIM, pltpu.roll(even, SB_DIM, 1), odd)


def _sb_prompt_kernel(q_ref, k_ref, v_ref, u_ref, o_ref, acc_ref, carry_ref):
    qi = pl.program_id(1)
    step = pl.program_id(2)
    kb = qi - step
    tq = q_ref.shape[1]
    tk = k_ref.shape[1]

    @pl.when(step == 0)
    def _():
        acc_ref[...] = jnp.zeros(acc_ref.shape, F32)
        carry_ref[...] = jnp.zeros(carry_ref.shape, F32)

    @pl.when(step <= qi)
    def _():
        row = qi * tq + lax.broadcasted_iota(jnp.int32, (tq, tk), 0)
        col = kb * tk + lax.broadcasted_iota(jnp.int32, (tq, tk), 1)
        visible = col < row
        for hd in range(SB_HEADS):
            z = _dot_nt(q_ref[0, :, hd * HEAD_PAD:(hd + 1) * HEAD_PAD], k_ref[0])
            pv, carry = _sb_block(z, visible, u_ref[...], carry_ref[hd], v_ref[0])
            acc_ref[hd] += pv
            carry_ref[hd] = carry

    @pl.when(step == qi)
    def _():
        per_group = SB_HEADS // SB_KV_HEADS
        for pair in range(SB_HEADS // 2):
            out = _sb_pair_out(acc_ref[2 * pair], acc_ref[2 * pair + 1], (2 * pair) // per_group)
            o_ref[0, :, pair * LANE:(pair + 1) * LANE] = out.astype(BF)


def _sb_prompt(q, k, v, *, blk):
    b, t, _ = q.shape
    n = t // blk
    u = _suffix_matrix(blk)
    kv_map = lambda i, qi, st: (i, jnp.maximum(qi - st, 0), 0)
    return pl.pallas_call(
        _sb_prompt_kernel,
        grid=(b, n, n),
        in_specs=[pl.BlockSpec((1, blk, q.shape[2]), lambda i, qi, st: (i, qi, 0)),
                  pl.BlockSpec((1, blk, k.shape[2]), kv_map),
                  pl.BlockSpec((1, blk, v.shape[2]), kv_map),
                  _const_spec(u.shape)],
        out_specs=pl.BlockSpec((1, blk, SB_HEADS * SB_DIM), lambda i, qi, st: (i, qi, 0)),
        out_shape=jax.ShapeDtypeStruct((b, t, SB_HEADS * SB_DIM), BF),
        scratch_shapes=[pltpu.VMEM((SB_HEADS, blk, LANE), F32)] * 2,
        compiler_params=_params("arbitrary", "arbitrary", "arbitrary"),
        name="sb_prompt",
    )(q, k, v, u)


def _decode_kernel(pt_ref, qlat_ref, qpe_ref, sbq_ref, latn_ref, kpen_ref, sbkn_ref, sbvn_ref, u_ref,
                   latc_ref, kpec_ref, sbkc_ref, sbvc_ref,
                   olat_ref, osb_ref,
                   lat_buf, kpe_buf, sbk_buf, sbv_buf, sems, m_ref, l_ref, acc_ref, sbacc_ref, carry_ref,
                   *, layer, n_pages, ppc):
    s_idx = pl.program_id(0)
    c_idx = pl.program_id(1)
    n_seq = pl.num_programs(0)
    n_chunks = pl.num_programs(1)
    step = s_idx * n_chunks + c_idx
    slot = lax.rem(step, 2)
    page = lat_buf.shape[2]
    rq = qlat_ref.shape[1]
    rows = rq * MLA_HEADS
    transposed = ((kpec_ref, kpe_buf), (sbkc_ref, sbk_buf), (sbvc_ref, sbv_buf))

    def copies(seq, chunk, slot_):
        base = seq * n_pages + (n_chunks - 1 - chunk) * ppc
        out = []
        for p in range(ppc):
            pg = pt_ref[base + p]
            out.append(pltpu.make_async_copy(latc_ref.at[layer, pg], lat_buf.at[slot_, p], sems.at[slot_, 0]))
            for a, (cache, buf) in enumerate(transposed):
                out.append(pltpu.make_async_copy(cache.at[layer, pg], buf.at[slot_, :, pl.ds(p * page, page)],
                                                 sems.at[slot_, a + 1]))
        return out

    @pl.when(step == 0)
    def _():
        for cp in copies(0, 0, 0):
            cp.start()

    @pl.when(step + 1 < n_seq * n_chunks)
    def _():
        last = c_idx + 1 == n_chunks
        nxt_seq = jnp.where(last, s_idx + 1, s_idx)
        nxt_chunk = jnp.where(last, 0, c_idx + 1)
        for cp in copies(nxt_seq, nxt_chunk, 1 - slot):
            cp.start()

    def head_rows(ref, width, take):
        return jnp.concatenate([ref[0, :, hd * width:hd * width + take] for hd in range(MLA_HEADS)], axis=0)

    qlat = head_rows(qlat_ref, KV_RANK, KV_RANK).astype(BF)
    qpe = head_rows(qpe_ref, HEAD_PAD, MLA_ROPE).astype(BF)
    sbq = head_rows(sbq_ref, HEAD_PAD, HEAD_PAD).astype(BF)

    def mla_update(s, kv):
        tk = s.shape[1]
        m_prev = m_ref[...]
        m_new = jnp.maximum(m_prev, jnp.max(s, axis=1, keepdims=True))
        alpha = jnp.exp(m_prev - m_new)
        p = jnp.exp(s - _lane_tile(m_new, tk // LANE))
        l_ref[...] = alpha * l_ref[...] + jnp.sum(p, axis=1, keepdims=True)
        m_ref[...] = m_new
        acc_ref[...] = _lane_tile(alpha, KV_RANK // LANE) * acc_ref[...] + _dot(p.astype(BF), kv)

    @pl.when(c_idx == 0)
    def _():
        m_ref[...] = jnp.full(m_ref.shape, -1e30, F32)
        l_ref[...] = jnp.zeros(l_ref.shape, F32)
        acc_ref[...] = jnp.zeros(acc_ref.shape, F32)
        pad = LANE - rq

        def padded(ref):
            v = ref[0]
            return jnp.concatenate([v, jnp.zeros((pad, v.shape[1]), F32)], axis=0).astype(BF)

        tok = lax.rem(lax.broadcasted_iota(jnp.int32, (rows, LANE), 0), rq)
        key = lax.broadcasted_iota(jnp.int32, (rows, LANE), 1)
        latn = padded(latn_ref)
        s = _dot_nt(qlat, latn) + _dot_nt(qpe, padded(kpen_ref))
        mla_update(jnp.where(key <= tok, s, -1e30), latn)
        z = _dot_nt(sbq, padded(sbkn_ref))
        pv, carry = _sb_block(z, key < tok, u_ref[0:LANE, 0:LANE], jnp.zeros((rows, LANE), F32), padded(sbvn_ref))
        sbacc_ref[...] = pv
        carry_ref[...] = carry

    for cp in copies(s_idx, c_idx, slot):
        cp.wait()

    tk = ppc * page
    latb = lat_buf[slot].reshape(tk, KV_RANK).astype(BF)
    mla_update(_dot_nt(qlat, latb) + _dot(qpe, kpe_buf[slot].astype(BF)), latb)

    sub = u_ref.shape[0]
    carry = carry_ref[...]
    sbacc = sbacc_ref[...]
    for sb in reversed(range(tk // sub)):
        kt = sbk_buf[slot, :, sb * sub:(sb + 1) * sub].astype(BF)
        vt = sbv_buf[slot, :, sb * sub:(sb + 1) * sub].astype(BF)
        pv, carry = _sb_block(_dot(sbq, kt), None, u_ref[...], carry, vt, v_transposed=True)
        sbacc = sbacc + pv
    carry_ref[...] = carry
    sbacc_ref[...] = sbacc

    @pl.when(c_idx == n_chunks - 1)
    def _():
        o = acc_ref[...] / _lane_tile(l_ref[...], KV_RANK // LANE)
        olat_ref[:, 0, :, :] = o.reshape(MLA_HEADS, rq, KV_RANK)
        acc = sbacc_ref[...]
        per_group = SB_HEADS // SB_KV_HEADS
        for pair in range(SB_HEADS // 2):
            even = acc[(2 * pair) * rq:(2 * pair + 1) * rq]
            odd = acc[(2 * pair + 1) * rq:(2 * pair + 2) * rq]
            osb_ref[0, :, pair * LANE:(pair + 1) * LANE] = _sb_pair_out(even, odd, (2 * pair) // per_group)


def _decode(page_table, qlat, qpe, sbq, latn, kpen, sbkn, sbvn, caches, *, layer):
    s, rq, _ = qlat.shape
    n_pages = page_table.shape[1]
    ppc = min(PAGES_PER_CHUNK, n_pages)
    latc, kpec, sbkc, sbvc = caches
    page = latc.shape[2]
    u = _suffix_matrix(2 * page)
    rows = rq * MLA_HEADS

    def seq_spec(c):
        return pl.BlockSpec((1, rq, c), lambda i, j, pt: (i, 0, 0))

    any_spec = pl.BlockSpec(memory_space=pl.ANY)
    grid_spec = pltpu.PrefetchScalarGridSpec(
        num_scalar_prefetch=1,
        grid=(s, n_pages // ppc),
        in_specs=[seq_spec(qlat.shape[2]), seq_spec(qpe.shape[2]), seq_spec(sbq.shape[2]),
                  seq_spec(latn.shape[2]), seq_spec(kpen.shape[2]), seq_spec(sbkn.shape[2]), seq_spec(sbvn.shape[2]),
                  pl.BlockSpec(u.shape, lambda i, j, pt: (0, 0)),
                  any_spec, any_spec, any_spec, any_spec],
        out_specs=[pl.BlockSpec((MLA_HEADS, 1, rq, KV_RANK), lambda i, j, pt: (0, i, 0, 0)),
                   pl.BlockSpec((1, rq, SB_HEADS * SB_DIM), lambda i, j, pt: (i, 0, 0))],
        scratch_shapes=[pltpu.VMEM((2, ppc, page, KV_RANK), F32),
                        pltpu.VMEM((2, MLA_ROPE, ppc * page), F32),
                        pltpu.VMEM((2, SB_KV_HEADS * SB_DIM, ppc * page), F32),
                        pltpu.VMEM((2, SB_KV_HEADS * SB_DIM, ppc * page), F32),
                        pltpu.SemaphoreType.DMA((2, 4)),
                        pltpu.VMEM((rows, LANE), F32),
                        pltpu.VMEM((rows, LANE), F32),
                        pltpu.VMEM((rows, KV_RANK), F32),
                        pltpu.VMEM((rows, LANE), F32),
                        pltpu.VMEM((rows, LANE), F32)])
    return pl.pallas_call(
        functools.partial(_decode_kernel, layer=layer, n_pages=n_pages, ppc=ppc),
        grid_spec=grid_spec,
        out_shape=[jax.ShapeDtypeStruct((MLA_HEADS, s, rq, KV_RANK), F32),
                   jax.ShapeDtypeStruct((s, rq, SB_HEADS * SB_DIM), F32)],
        compiler_params=_params("arbitrary", "arbitrary"),
        name="decode_attention",
    )(page_table.reshape(-1), qlat, qpe, sbq, latn, kpen, sbkn, sbvn, u, latc, kpec, sbkc, sbvc)


def _merge_kernel(*refs, absorbed):
    if absorbed:
        (a_ref, olat_ref, osb_ref, gate_ref, x_ref, mod_ref, wuv_ref,
         wco_ref, wmo_ref, wso_ref, wout_ref, o_ref) = refs
    else:
        (a_ref, omla_ref, osb_ref, gate_ref, x_ref, mod_ref,
         wco_ref, wmo_ref, wso_ref, wout_ref, o_ref) = refs
    ts, tr, d = x_ref.shape
    tm = ts * tr

    def rows2(ref):
        return ref[...].reshape(tm, ref.shape[-1])

    if absorbed:
        omla = jnp.zeros((tm, MLA_HEADS * MLA_V), F32)
        for hd in range(MLA_HEADS):
            omla = omla + _dot(olat_ref[hd].reshape(tm, KV_RANK).astype(BF), wuv_ref[hd])
        omla = omla.astype(BF)
    else:
        omla = rows2(omla_ref)
    gate = rows2(gate_ref)
    merged = (gate[:, 0:d] * _dot(rows2(a_ref), wco_ref[...])
              + gate[:, d:2 * d] * _dot(omla, wmo_ref[...])
              + gate[:, 2 * d:3 * d] * _dot(rows2(osb_ref).astype(BF), wso_ref[...]))
    mix = _dot(merged.astype(BF), wout_ref[...])
    o_ref[...] = x_ref[...] + mod_ref[...][:, 2:3, :] * mix.reshape(ts, tr, d)


def _merge(a, omla, osb, gate, x, mod, wl, *, absorbed, ts, tr):
    s, r, d = x.shape

    def row_spec(c):
        return pl.BlockSpec((ts, tr, c), lambda i, j: (i, j, 0))

    if absorbed:
        mla_spec = pl.BlockSpec((MLA_HEADS, ts, tr, KV_RANK), lambda i, j: (0, i, j, 0))
        weights = [wl["w_uv_pad"]]
    else:
        mla_spec = row_spec(omla.shape[2])
        weights = []
    weights += [wl["w_conv_out"], wl["w_mla_out"], wl["w_sb_out"], wl["w_out"]]
    return pl.pallas_call(
        functools.partial(_merge_kernel, absorbed=absorbed),
        grid=(s // ts, r // tr),
        in_specs=[row_spec(a.shape[2]), mla_spec, row_spec(osb.shape[2]), row_spec(gate.shape[2]), row_spec(d),
                  pl.BlockSpec((ts, 6, d), lambda i, j: (i, 0, 0))] + [_const_spec(w.shape) for w in weights],
        out_specs=row_spec(d),
        out_shape=jax.ShapeDtypeStruct((s, r, d), F32),
        compiler_params=_params("arbitrary", "arbitrary"),
        name="merge_sample" if absorbed else "merge_prompt",
    )(a, omla, osb, gate, x, mod, *weights)


def _ffn_kernel(*refs, final):
    (x_ref, mod_ref, ctx_ref, gn_ref, wup_ref, wdw_ref, bdw_ref, wdown_ref) = refs[:8]
    if final:
        gf_ref, o_ref, st_ref, y_ref, carry_ref, win_ref, f_ref = refs[8:]
    else:
        o_ref, st_ref, carry_ref, win_ref, f_ref = refs[8:]
    ts, tr, d = x_ref.shape
    tm = ts * tr
    pad = SUBLANE
    ctx_rows = FFN_K - 1

    @pl.when(pl.program_id(1) == 0)
    def _():
        carry_ref[...] = ctx_ref[...]

    x = x_ref[...]
    mod = mod_ref[...]
    h = _rms(x, gn_ref[...]) * (1.0 + mod[:, 4:5, :]) + mod[:, 3:4, :]
    hb = h.reshape(tm, d).astype(BF)

    def conv_chunk(c0):
        up = _dot(hb, wup_ref[:, c0:c0 + FFN_CHUNK]).reshape(ts, tr, FFN_CHUNK)
        win_ref[:, pad:pad + tr, :] = up
        win_ref[:, pad - ctx_rows:pad, :] = carry_ref[:, :, c0:c0 + FFN_CHUNK]
        y = bdw_ref[:, c0:c0 + FFN_CHUNK][None] + wdw_ref[2:3, c0:c0 + FFN_CHUNK][None] * up
        for k in range(ctx_rows):
            y = y + wdw_ref[k:k + 1, c0:c0 + FFN_CHUNK][None] * win_ref[:, pad - ctx_rows + k:pad - ctx_rows + k + tr, :]
        carry_ref[:, :, c0:c0 + FFN_CHUNK] = win_ref[:, pad + tr - ctx_rows:pad + tr, :]
        return y.reshape(tm, FFN_CHUNK)

    for i in range(D_FF // FFN_CHUNK):
        a = conv_chunk(i * FFN_CHUNK)
        g = conv_chunk(D_FF + i * FFN_CHUNK)
        act = (a * jax.nn.sigmoid(a) * g).astype(BF)
        part = _dot(act, wdown_ref[i * FFN_CHUNK:(i + 1) * FFN_CHUNK, :])
        if i == 0:
            f_ref[...] = part
        else:
            f_ref[...] += part
    st_ref[...] = carry_ref[...]
    out = x + mod[:, 5:6, :] * f_ref[...].reshape(ts, tr, d)
    o_ref[...] = out
    if final:
        y_ref[...] = _rms(out, gf_ref[...])


def _ffn(x, mod, ctx, wl, g_final, *, final, ts, tr):
    s, r, d = x.shape

    def row_spec():
        return pl.BlockSpec((ts, tr, d), lambda i, j: (i, j, 0))

    state_spec = pl.BlockSpec((ts, FFN_K - 1, 2 * D_FF), lambda i, j: (i, 0, 0))
    weights = [wl["g_norm_ffn"], wl["w_ffn_up"], wl["w_ffn_dw"], wl["b_ffn_dw"], wl["w_ffn_down"]]
    if final:
        weights.append(g_final)
    out_specs = [row_spec(), state_spec]
    out_shape = [jax.ShapeDtypeStruct((s, r, d), F32), jax.ShapeDtypeStruct((s, FFN_K - 1, 2 * D_FF), F32)]
    if final:
        out_specs.append(row_spec())
        out_shape.append(jax.ShapeDtypeStruct((s, r, d), F32))
    return pl.pallas_call(
        functools.partial(_ffn_kernel, final=final),
        grid=(s // ts, r // tr),
        in_specs=[row_spec(), pl.BlockSpec((ts, 6, d), lambda i, j: (i, 0, 0)), state_spec]
        + [_const_spec(w.shape) for w in weights],
        out_specs=out_specs,
        out_shape=out_shape,
        scratch_shapes=[pltpu.VMEM((ts, FFN_K - 1, 2 * D_FF), F32),
                        pltpu.VMEM((ts, tr + SUBLANE, FFN_CHUNK), F32),
                        pltpu.VMEM((ts * tr, d), F32)],
        compiler_params=_params("arbitrary", "arbitrary"),
        name="ffn_final" if final else "ffn",
    )(x, mod, ctx, *weights)


def _pad_last(w, n):
    return jnp.pad(w, [(0, 0)] * (w.ndim - 1) + [(0, n - w.shape[-1])])


def _prep_weights(w_in, w_uq, w_uk, w_uv):
    depth, d, _ = w_in.shape
    cuts = np.cumsum(IN_SIZES)[:-1]
    ua, ub, cq, ckv, kpe, sbq, sbk, sbv, gate = jnp.split(w_in, cuts, axis=-1)
    half = MLA_ROPE // 2
    rot = lambda w: jnp.concatenate([-w[..., half:], w[..., :half]], axis=-1)
    rep = LANE // MLA_ROPE
    kpe_t = jnp.tile(kpe, (1, 1, rep))
    kper_t = jnp.tile(rot(kpe), (1, 1, rep))
    sbq4 = sbq.reshape(depth, d, SB_HEADS, SB_DIM)
    per_group = SB_HEADS // SB_KV_HEADS
    in_g0 = (jnp.arange(SB_HEADS) < per_group)[None, None, :, None]
    sbq_pad = jnp.concatenate([jnp.where(in_g0, sbq4, 0.0), jnp.where(in_g0, 0.0, sbq4)], axis=-1)
    sbq_pad = sbq_pad.reshape(depth, d, SB_HEADS * HEAD_PAD)
    w_main = jnp.concatenate([ua, ub, cq, ckv, kpe_t, kper_t, sbq_pad, sbk, sbv, gate], axis=-1).astype(BF)

    wq4 = w_uq.reshape(depth, Q_RANK, MLA_HEADS, MLA_NOPE + MLA_ROPE)
    nope, pe = wq4[..., :MLA_NOPE], wq4[..., MLA_NOPE:]
    w_q = _pad_last(wq4, HEAD_PAD).reshape(depth, Q_RANK, -1).astype(BF)
    w_qr = _pad_last(jnp.concatenate([jnp.zeros_like(nope), rot(pe)], axis=-1), HEAD_PAD)
    w_qr = w_qr.reshape(depth, Q_RANK, -1).astype(BF)
    w_qpe = _pad_last(pe, HEAD_PAD).reshape(depth, Q_RANK, -1).astype(BF)
    w_qper = _pad_last(rot(pe), HEAD_PAD).reshape(depth, Q_RANK, -1).astype(BF)
    w_k = _pad_last(w_uk, HEAD_PAD).reshape(depth, KV_RANK, -1).astype(BF)
    w_v = w_uv.reshape(depth, KV_RANK, -1).astype(BF)
    w_ukt = jnp.transpose(w_uk, (0, 2, 3, 1))
    w_ukt = jnp.pad(w_ukt, ((0, 0), (0, 0), (0, HEAD_PAD - MLA_NOPE), (0, 0))).astype(BF)
    eye = jnp.eye(MLA_HEADS, dtype=w_uv.dtype)
    w_uv_pad = jnp.einsum("lrhv,hg->lhrgv", w_uv, eye).reshape(depth, MLA_HEADS, KV_RANK, MLA_HEADS * MLA_V)
    return dict(w_main=w_main, w_q=w_q, w_qr=w_qr, w_qpe=w_qpe, w_qper=w_qper, w_k=w_k, w_v=w_v,
                w_ukt=w_ukt, w_uv_pad=w_uv_pad.astype(BF))


def _rope_tables(pos):
    half = MLA_ROPE // 2
    inv_freq = ROPE_THETA ** (-jnp.arange(half, dtype=F32) / half)
    ang = pos.astype(F32)[:, None] * inv_freq[None, :]
    cos, sin = jnp.cos(ang), jnp.sin(ang)
    n = pos.shape[0]
    cos2 = jnp.concatenate([cos, cos], axis=-1)
    sin2 = jnp.concatenate([sin, sin], axis=-1)
    ones, zeros = jnp.ones((n, MLA_NOPE), F32), jnp.zeros((n, MLA_NOPE), F32)
    rep = LANE // MLA_ROPE
    return jnp.stack([
        MLA_SCALE * _pad_last(jnp.concatenate([ones, cos2], axis=-1), LANE),
        MLA_SCALE * _pad_last(jnp.concatenate([zeros, sin2], axis=-1), LANE),
        jnp.tile(cos2, (1, rep)),
        jnp.tile(sin2, (1, rep)),
        MLA_SCALE * _pad_last(cos2, LANE),
        MLA_SCALE * _pad_last(sin2, LANE)])


def kernel(x_prompt, x_sample, c_prompt, c_sample, cache_mla_latent, cache_mla_krope, cache_sb_k, cache_sb_v, state_conv, state_ffn_conv, page_table, w_ada, b_ada, g_norm_mix, g_norm_ffn, w_in, b_gate, w_dw, b_dw, g_conv_ln, b_conv_ln, w_conv_out, g_q, w_uq, g_kv, w_uk, w_uv, w_mla_out, w_sb_out, w_out, w_ffn_up, w_ffn_dw, b_ffn_dw, w_ffn_down, g_final):
    depth = w_in.shape[0]
    b, t, d = x_prompt.shape
    s, r, _ = x_sample.shape
    page = cache_mla_latent.shape[2]
    past_len = page_table.shape[1] * page
    assert r == SUBLANE and t % ATT_BLOCK == 0 and t % SB_BLOCK == 0 and t % ROW_TILE == 0
    assert s % SEQS_PER_TILE == 0 and page == LANE

    prepped = _prep_weights(w_in, w_uq, w_uk, w_uv)
    row = lambda w: w.reshape(depth, 1, w.shape[-1])
    stacked = dict(
        prepped,
        g_norm_mix=row(g_norm_mix), g_norm_ffn=row(g_norm_ffn), g_q=row(g_q), g_kv=row(g_kv),
        b_gate=row(b_gate), w_dw=w_dw, b_dw=row(b_dw), g_conv_ln=row(g_conv_ln), b_conv_ln=row(b_conv_ln),
        w_conv_out=w_conv_out.astype(BF), w_mla_out=w_mla_out.astype(BF), w_sb_out=w_sb_out.astype(BF),
        w_out=w_out.astype(BF), w_ffn_up=w_ffn_up.astype(BF), w_ffn_dw=w_ffn_dw, b_ffn_dw=row(b_ffn_dw),
        w_ffn_down=w_ffn_down.astype(BF))
    g_fin = g_final.reshape(1, d)

    tab_p = _rope_tables(jnp.arange(t, dtype=jnp.int32))
    tab_s = _rope_tables(past_len + jnp.arange(r, dtype=jnp.int32))

    n_mod = -(-(b + s) // SUBLANE) * SUBLANE
    c_all = jnp.concatenate([c_prompt, c_sample, jnp.zeros((n_mod - b - s, d), F32)], axis=0)
    mods = _ada(c_all, w_ada, b_ada)

    feat_major = lambda c: jnp.transpose(c, (0, 1, 3, 4, 2)).reshape(c.shape[:2] + (-1, c.shape[2]))
    caches = (cache_mla_latent, jnp.swapaxes(cache_mla_krope, 2, 3), feat_major(cache_sb_k), feat_major(cache_sb_v))
    ffn_zero = jnp.zeros((b, FFN_K - 1, 2 * D_FF), F32)

    xp, xs = x_prompt, x_sample
    yp = ys = None
    st_p, st_s = [], []
    for l in range(depth):
        wl = {k: v[l] for k, v in stacked.items()}
        final = l == depth - 1
        mod_p = mods[l, :b].reshape(b, 6, d)
        mod_s = mods[l, b:b + s].reshape(s, 6, d)

        (u, q, lat, kpe, sbq, sbk, sbv, gate, k, v, sbkb, sbvb) = _inproj(
            xp, mod_p, tab_p, wl, prompt=True, ts=1, tr=ROW_TILE)
        a = _conv_prompt(u, wl, tm=ROW_TILE)
        omla = _mla_prompt(q, k, v, blk=ATT_BLOCK)
        osb = _sb_prompt(sbq, sbkb, sbvb, blk=SB_BLOCK)
        x1 = _merge(a, omla, osb, gate, xp, mod_p, wl, absorbed=False, ts=1, tr=ROW_TILE)
        res = _ffn(x1, mod_p, ffn_zero, wl, g_fin, final=final, ts=1, tr=ROW_TILE)
        xp, ffn_state = res[0], res[1]
        if final:
            yp = res[2]
        st_p.append((lat, kpe, sbk.reshape(b, t, SB_KV_HEADS, SB_DIM), sbv.reshape(b, t, SB_KV_HEADS, SB_DIM),
                     u[:, t - (CONV_K - 1):], ffn_state))

        (u, lat, kpe, sbq, sbk, sbv, gate, qlat, qpe) = _inproj(
            xs, mod_s, tab_s, wl, prompt=False, ts=SEQS_PER_TILE, tr=r)
        full = jnp.concatenate([state_conv[l], u], axis=1)
        a = _conv_sample(full, wl, ts=SEQS_PER_TILE, tr=r)
        olat, osb = _decode(page_table, qlat, qpe, sbq, lat, kpe, sbk, sbv, caches, layer=l)
        x1 = _merge(a, olat, osb, gate, xs, mod_s, wl, absorbed=True, ts=SEQS_PER_TILE, tr=r)
        res = _ffn(x1, mod_s, state_ffn_conv[l], wl, g_fin, final=final, ts=SEQS_PER_TILE, tr=r)
        xs, ffn_state = res[0], res[1]
        if final:
            ys = res[2]
        st_s.append((lat, kpe, sbk.reshape(s, r, SB_KV_HEADS, SB_DIM), sbv.reshape(s, r, SB_KV_HEADS, SB_DIM),
                     full[:, r:], ffn_state))

    stack = lambda sts, i: jnp.stack([st[i] for st in sts], axis=0)
    return (yp, ys,
            stack(st_p, 0), stack(st_p, 1), stack(st_p, 2), stack(st_p, 3), stack(st_p, 4), stack(st_p, 5),
            stack(st_s, 0), stack(st_s, 1), stack(st_s, 2), stack(st_s, 3), stack(st_s, 4), stack(st_s, 5))
```

```python
import functools

import jax
import jax.numpy as jnp
import numpy as np
from jax import lax
from jax.experimental import pallas as pl
from jax.experimental.pallas import tpu as pltpu

F32 = jnp.float32
BF = jnp.bfloat16

D_MODEL = 1024
D_CONV = 512
CONV_K = 31
MLA_HEADS = 8
MLA_NOPE = 64
MLA_ROPE = 32
MLA_V = 64
Q_RANK = 384
KV_RANK = 256
ROPE_THETA = 10000.0
MLA_SCALE = (MLA_NOPE + MLA_ROPE) ** -0.5
SB_HEADS = 8
SB_KV_HEADS = 2
SB_DIM = 64
SB_SCALE = SB_DIM ** -0.5
D_FF = 2816
FFN_K = 3
N_BRANCH = 3
EPS = 1e-6
IN_SIZES = (D_CONV, D_CONV, Q_RANK, KV_RANK, MLA_ROPE, SB_HEADS * SB_DIM,
            SB_KV_HEADS * SB_DIM, SB_KV_HEADS * SB_DIM, N_BRANCH * D_MODEL)

LANE = 128
SUBLANE = 8
VMEM_LIMIT = 56 * 1024 * 1024

HEAD_PAD = LANE
C_UA = 0
C_UB = C_UA + D_CONV
C_CQ = C_UB + D_CONV
C_CKV = C_CQ + Q_RANK
C_KPE = C_CKV + KV_RANK
C_KPER = C_KPE + LANE
C_SBQ = C_KPER + LANE
C_SBK = C_SBQ + SB_HEADS * HEAD_PAD
C_SBV = C_SBK + SB_KV_HEADS * SB_DIM
C_GATE = C_SBV + SB_KV_HEADS * SB_DIM
C_END = C_GATE + N_BRANCH * D_MODEL
GATE_CHUNK = 512

ROW_TILE = 256
ATT_BLOCK = 512
SB_BLOCK = 256
CONV_ROWS = 64
FFN_CHUNK = 256
PAGES_PER_CHUNK = 16
SEQS_PER_TILE = ROW_TILE // SUBLANE


def _params(*sem):
    return pltpu.CompilerParams(dimension_semantics=sem, vmem_limit_bytes=VMEM_LIMIT)


def _const_spec(shape):
    nd = len(shape)
    return pl.BlockSpec(shape, lambda *_: (0,) * nd)


def _rms(v, g):
    return v * lax.rsqrt(jnp.mean(v * v, axis=-1, keepdims=True) + EPS) * g


def _dot(a, b):
    return jnp.dot(a, b, preferred_element_type=F32)


def _dot_nt(a, b):
    return lax.dot_general(a, b, (((1,), (1,)), ((), ())), preferred_element_type=F32)


def _lane_tile(v, n):
    return v if n == 1 else jnp.concatenate([v] * n, axis=-1)


def _ada_kernel(c_ref, w_ref, b_ref, o_ref):
    c = c_ref[...]
    a = (c * jax.nn.sigmoid(c)).astype(BF)
    o_ref[0] = _dot(a, w_ref[0].astype(BF)) + b_ref[0]


def _ada(c_all, w_ada, b_ada):
    depth, d, n = w_ada.shape
    rows = c_all.shape[0]
    tn = 512
    return pl.pallas_call(
        _ada_kernel,
        grid=(depth, n // tn),
        in_specs=[pl.BlockSpec((rows, d), lambda l, j: (0, 0)),
                  pl.BlockSpec((1, d, tn), lambda l, j: (l, 0, j)),
                  pl.BlockSpec((1, 1, tn), lambda l, j: (l, 0, j))],
        out_specs=pl.BlockSpec((1, rows, tn), lambda l, j: (l, 0, j)),
        out_shape=jax.ShapeDtypeStruct((depth, rows, n), F32),
        compiler_params=_params("arbitrary", "arbitrary"),
        name="ada",
    )(c_all, w_ada, b_ada.reshape(depth, 1, n))


def _inproj_kernel(*refs, prompt):
    (x_ref, mod_ref, tab_ref, gn_ref, wmain_ref, gq_ref, wq_ref, wqr_ref, gkv_ref, bgate_ref) = refs[:10]
    if prompt:
        wk_ref, wv_ref = refs[10:12]
        (u_ref, q_ref, lat_ref, kpe_ref, sbq_ref, sbk_ref, sbv_ref, gate_ref,
         k_ref, v_ref, sbkb_ref, sbvb_ref) = refs[12:]
    else:
        wukt_ref, wqpe_ref, wqper_ref = refs[10:13]
        (u_ref, lat_ref, kpe_ref, sbq_ref, sbk_ref, sbv_ref, gate_ref,
         qlat_ref, qpe_ref) = refs[13:]
    ts, tr, d = x_ref.shape
    tm = ts * tr

    def rows3(v):
        return v.reshape(ts, tr, v.shape[-1])

    mod = mod_ref[...]
    h = _rms(x_ref[...], gn_ref[...]) * (1.0 + mod[:, 1:2, :]) + mod[:, 0:1, :]
    hb = h.reshape(tm, d).astype(BF)

    def seg(a, b):
        return _dot(hb, wmain_ref[:, a:b])

    tab = tab_ref[...]

    u_ref[...] = rows3(seg(C_UA, C_UB) * jax.nn.sigmoid(seg(C_UB, C_CQ)))

    cqn = _rms(seg(C_CQ, C_CKV), gq_ref[...]).astype(BF)
    qf = (rows3(_dot(cqn, wq_ref[...])) * _lane_tile(tab[0], MLA_HEADS)[None]
          + rows3(_dot(cqn, wqr_ref[...])) * _lane_tile(tab[1], MLA_HEADS)[None])
    if prompt:
        q_ref[...] = qf.astype(BF)

    lat = _rms(seg(C_CKV, C_KPE), gkv_ref[...])
    lat_ref[...] = rows3(lat)
    kr = rows3(seg(C_KPE, C_KPER)) * tab[2][None] + rows3(seg(C_KPER, C_SBQ)) * tab[3][None]
    kpe_ref[...] = kr[:, :, :MLA_ROPE]

    if prompt:
        latb = lat.astype(BF)
        lane = lax.broadcasted_iota(jnp.int32, (1, 1, LANE), 2)
        kr_head = jnp.where((lane >= MLA_NOPE) & (lane < MLA_NOPE + MLA_ROPE), kr, 0.0)
        k_ref[...] = (rows3(_dot(latb, wk_ref[...])) + _lane_tile(kr_head, MLA_HEADS)).astype(BF)
        v_ref[...] = rows3(_dot(latb, wv_ref[...])).astype(BF)
    else:
        qfb = qf.reshape(tm, MLA_HEADS * HEAD_PAD).astype(BF)
        for hd in range(MLA_HEADS):
            ql = _dot(qfb[:, hd * HEAD_PAD:(hd + 1) * HEAD_PAD], wukt_ref[hd])
            qlat_ref[:, :, hd * KV_RANK:(hd + 1) * KV_RANK] = rows3(ql)
        qpe_ref[...] = (rows3(_dot(cqn, wqpe_ref[...])) * _lane_tile(tab[4], MLA_HEADS)[None]
                        + rows3(_dot(cqn, wqper_ref[...])) * _lane_tile(tab[5], MLA_HEADS)[None])

    sbq = seg(C_SBQ, C_SBK) * SB_SCALE
    if prompt:
        for hd in range(SB_HEADS):
            sbq_ref[0, hd] = sbq[:, hd * HEAD_PAD:(hd + 1) * HEAD_PAD].astype(BF)
    else:
        sbq_ref[...] = rows3(sbq)
    sbk = rows3(seg(C_SBK, C_SBV))
    sbv = rows3(seg(C_SBV, C_GATE))
    sbk_ref[...] = sbk
    sbv_ref[...] = sbv
    if prompt:
        sbkb_ref[...] = sbk.astype(BF)
        sbvb_ref[...] = sbv.astype(BF)

    for c in range(0, N_BRANCH * D_MODEL, GATE_CHUNK):
        g = jax.nn.sigmoid(seg(C_GATE + c, C_GATE + c + GATE_CHUNK) + bgate_ref[:, c:c + GATE_CHUNK])
        gate_ref[:, :, c:c + GATE_CHUNK] = rows3(g).astype(BF)


def _inproj(x, mod, tab, wl, *, prompt, ts, tr):
    s, r, d = x.shape
    grid = (s // ts, r // tr)

    def row_spec(c):
        return pl.BlockSpec((ts, tr, c), lambda i, j: (i, j, 0))

    def row_shape(c, dt):
        return jax.ShapeDtypeStruct((s, r, c), dt)

    weights = [wl["g_norm_mix"], wl["w_main"], wl["g_q"], wl["w_q"], wl["w_qr"], wl["g_kv"], wl["b_gate"]]
    if prompt:
        weights += [wl["w_k"], wl["w_v"]]
    else:
        weights += [wl["w_ukt"], wl["w_qpe"], wl["w_qper"]]
    in_specs = [row_spec(d),
                pl.BlockSpec((ts, 6, d), lambda i, j: (i, 0, 0)),
                pl.BlockSpec((6, tr, LANE), lambda i, j: (0, j, 0))]
    in_specs += [_const_spec(w.shape) for w in weights]
    outs = [(D_CONV, F32)] + ([(MLA_HEADS * HEAD_PAD, BF)] if prompt else [])
    outs += [(KV_RANK, F32), (MLA_ROPE, F32), (SB_HEADS * HEAD_PAD, BF if prompt else F32),
             (SB_KV_HEADS * SB_DIM, F32), (SB_KV_HEADS * SB_DIM, F32), (N_BRANCH * D_MODEL, BF)]
    if prompt:
        outs += [(MLA_HEADS * HEAD_PAD, BF), (MLA_HEADS * MLA_V, BF),
                 (SB_KV_HEADS * SB_DIM, BF), (SB_KV_HEADS * SB_DIM, BF)]
    else:
        outs += [(MLA_HEADS * KV_RANK, F32), (MLA_HEADS * HEAD_PAD, F32)]
    out_specs = [row_spec(c) for c, _ in outs]
    out_shape = [row_shape(c, dt) for c, dt in outs]
    if prompt:
        assert ts == 1
        sbq_at = 4
        out_specs[sbq_at] = pl.BlockSpec((1, SB_HEADS, tr, HEAD_PAD), lambda i, j: (i, 0, j, 0))
        out_shape[sbq_at] = jax.ShapeDtypeStruct((s, SB_HEADS, r, HEAD_PAD), BF)
    return pl.pallas_call(
        functools.partial(_inproj_kernel, prompt=prompt),
        grid=grid,
        in_specs=in_specs,
        out_specs=out_specs,
        out_shape=out_shape,
        compiler_params=_params("arbitrary", "arbitrary"),
        name="inproj_prompt" if prompt else "inproj_sample",
    )(x, mod, tab, *weights)


def _ln_swish(acc, g, b):
    xc = acc - jnp.mean(acc, axis=-1, keepdims=True)
    y = xc * lax.rsqrt(jnp.mean(xc * xc, axis=-1, keepdims=True) + EPS) * g + b
    return y * jax.nn.sigmoid(y)


def _conv_prompt_kernel(cur_ref, prev_ref, w_ref, b_ref, g_ref, bl_ref, o_ref, win_ref):
    tm = cur_ref.shape[1]
    halo = prev_ref.shape[1]
    j = pl.program_id(1)
    win_ref[0:halo, :] = jnp.where(j > 0, prev_ref[0], 0.0)
    win_ref[halo:, :] = cur_ref[0]
    off = halo - (CONV_K - 1)
    for r0 in range(0, tm, CONV_ROWS):
        acc = jnp.broadcast_to(b_ref[...], (CONV_ROWS, D_CONV))
        for k in range(CONV_K):
            acc = acc + w_ref[k:k + 1, :] * win_ref[off + r0 + k:off + r0 + k + CONV_ROWS, :]
        o_ref[0, r0:r0 + CONV_ROWS, :] = _ln_swish(acc, g_ref[...], bl_ref[...]).astype(BF)


def _conv_prompt(u, wl, *, tm):
    b, t, c = u.shape
    halo = 32
    ratio = tm // halo
    return pl.pallas_call(
        _conv_prompt_kernel,
        grid=(b, t // tm),
        in_specs=[pl.BlockSpec((1, tm, c), lambda i, j: (i, j, 0)),
                  pl.BlockSpec((1, halo, c), lambda i, j: (i, jnp.maximum(j * ratio - 1, 0), 0)),
                  _const_spec(wl["w_dw"].shape), _const_spec(wl["b_dw"].shape),
                  _const_spec(wl["g_conv_ln"].shape), _const_spec(wl["b_conv_ln"].shape)],
        out_specs=pl.BlockSpec((1, tm, c), lambda i, j: (i, j, 0)),
        out_shape=jax.ShapeDtypeStruct((b, t, c), BF),
        scratch_shapes=[pltpu.VMEM((tm + halo, c), F32)],
        compiler_params=_params("arbitrary", "arbitrary"),
        name="conv_prompt",
    )(u, u, wl["w_dw"], wl["b_dw"], wl["g_conv_ln"], wl["b_conv_ln"])


def _conv_sample_kernel(full_ref, w_ref, b_ref, g_ref, bl_ref, o_ref):
    ts, _, c = full_ref.shape
    tr = o_ref.shape[1]
    acc = jnp.broadcast_to(b_ref[...][None], (ts, tr, c))
    for k in range(CONV_K):
        acc = acc + w_ref[k:k + 1, :][None] * full_ref[:, k:k + tr, :]
    o_ref[...] = _ln_swish(acc, g_ref[...][None], bl_ref[...][None]).astype(BF)


def _conv_sample(full, wl, *, ts, tr):
    s, rows, c = full.shape
    return pl.pallas_call(
        _conv_sample_kernel,
        grid=(s // ts,),
        in_specs=[pl.BlockSpec((ts, rows, c), lambda i: (i, 0, 0)),
                  _const_spec(wl["w_dw"].shape), _const_spec(wl["b_dw"].shape),
                  _const_spec(wl["g_conv_ln"].shape), _const_spec(wl["b_conv_ln"].shape)],
        out_specs=pl.BlockSpec((ts, tr, c), lambda i: (i, 0, 0)),
        out_shape=jax.ShapeDtypeStruct((s, tr, c), BF),
        compiler_params=_params("arbitrary"),
        name="conv_sample",
    )(full, wl["w_dw"], wl["b_dw"], wl["g_conv_ln"], wl["b_conv_ln"])


def _triangle(n, descending):
    qi = np.concatenate([np.full(i + 1, i) for i in range(n)])
    kb = np.concatenate([np.arange(i, -1, -1) if descending else np.arange(i + 1) for i in range(n)])
    return jnp.asarray(qi, jnp.int32), jnp.asarray(kb, jnp.int32)


def _mla_prompt_kernel(qi_ref, kb_ref, q_ref, k_ref, v_ref, o_ref, m_ref, l_ref, acc_ref):
    pair_idx = pl.program_id(1)
    qi = qi_ref[pair_idx]
    kb = kb_ref[pair_idx]
    tq = q_ref.shape[1]
    tk = k_ref.shape[1]

    @pl.when(kb == 0)
    def _():
        m_ref[...] = jnp.full(m_ref.shape, -1e30, F32)
        l_ref[...] = jnp.zeros(l_ref.shape, F32)
        acc_ref[...] = jnp.zeros(acc_ref.shape, F32)

    def update(diagonal):
        if diagonal:
            visible = (lax.broadcasted_iota(jnp.int32, (tq, tk), 1)
                       <= lax.broadcasted_iota(jnp.int32, (tq, tk), 0))
        for hd in range(MLA_HEADS):
            s = _dot_nt(q_ref[0, :, hd * HEAD_PAD:(hd + 1) * HEAD_PAD],
                        k_ref[0, :, hd * HEAD_PAD:(hd + 1) * HEAD_PAD])
            if diagonal:
                s = jnp.where(visible, s, -1e30)
            m_prev = m_ref[hd]
            m_new = jnp.maximum(m_prev, jnp.max(s, axis=1, keepdims=True))
            alpha = jnp.exp(m_prev - m_new)
            p = jnp.exp(s - _lane_tile(m_new, tk // LANE))
            l_ref[hd] = alpha * l_ref[hd] + jnp.sum(p, axis=1, keepdims=True)
            m_ref[hd] = m_new
            pair = hd // 2
            acc_ref[hd] = alpha * acc_ref[hd] + _dot(p.astype(BF), v_ref[0, :, pair * LANE:(pair + 1) * LANE])

    @pl.when(kb < qi)
    def _():
        update(False)

    @pl.when(kb == qi)
    def _():
        update(True)
        lane = lax.broadcasted_iota(jnp.int32, (tq, LANE), 1)
        for pair in range(MLA_HEADS // 2):
            even = acc_ref[2 * pair] / l_ref[2 * pair]
            odd = acc_ref[2 * pair + 1] / l_ref[2 * pair + 1]
            o_ref[0, :, pair * LANE:(pair + 1) * LANE] = jnp.where(lane < MLA_V, even, odd).astype(BF)


def _mla_prompt(q, k, v, *, blk):
    b, t, _ = q.shape
    qi, kb = _triangle(t // blk, descending=False)
    grid_spec = pltpu.PrefetchScalarGridSpec(
        num_scalar_prefetch=2,
        grid=(b, qi.shape[0]),
        in_specs=[pl.BlockSpec((1, blk, q.shape[2]), lambda i, p, qi, kb: (i, qi[p], 0)),
                  pl.BlockSpec((1, blk, k.shape[2]), lambda i, p, qi, kb: (i, kb[p], 0)),
                  pl.BlockSpec((1, blk, v.shape[2]), lambda i, p, qi, kb: (i, kb[p], 0))],
        out_specs=pl.BlockSpec((1, blk, v.shape[2]), lambda i, p, qi, kb: (i, qi[p], 0)),
        scratch_shapes=[pltpu.VMEM((MLA_HEADS, blk, LANE), F32)] * 3)
    return pl.pallas_call(
        _mla_prompt_kernel,
        grid_spec=grid_spec,
        out_shape=jax.ShapeDtypeStruct((b, t, v.shape[2]), BF),
        compiler_params=_params("arbitrary", "arbitrary"),
        name="mla_prompt",
    )(qi, kb, q, k, v)


def _neg_suffix_matrix(n):
    s = np.arange(n)[:, None]
    j = np.arange(n)[None, :]
    u = -(s > j).astype(np.float32)
    return jnp.asarray(np.concatenate([u, u], axis=0), dtype=BF)


def _sb_block(z, visible, uneg, carry, v, v_transposed=False):
    tq, tk = z.shape
    sub = uneg.shape[1]
    nb = tk // sub
    lg = jnp.log(1.0 + jnp.exp(-jnp.abs(z)))
    sp = jnp.maximum(z, 0.0) + lg
    if visible is not None:
        sp = jnp.where(visible, sp, 0.0)
    log_beta = jnp.minimum(z, 0.0) - lg
    hi = sp.astype(BF)
    lo = (sp - hi.astype(F32)).astype(BF)
    cols = lambda x, i: x if nb == 1 else x[:, i * sub:(i + 1) * sub]
    split = [jnp.concatenate([cols(hi, i), cols(lo, i)], axis=1) for i in range(nb)]
    r = _dot(split[0] if nb == 1 else jnp.concatenate(split, axis=0), uneg)
    inner = [r[i * tq:(i + 1) * tq] for i in range(nb)]
    pieces = [None] * nb
    for i in reversed(range(nb)):
        pieces[i] = inner[i] + _lane_tile(carry, sub // LANE)
        carry = carry - jnp.sum(cols(sp, i), axis=1, keepdims=True)
    log_after = pieces[0] if nb == 1 else jnp.concatenate(pieces, axis=1)
    a = jnp.exp(log_beta + log_after)
    if visible is not None:
        a = jnp.where(visible, a, 0.0)
    pv = _dot_nt(a.astype(BF), v) if v_transposed else _dot(a.astype(BF), v)
    return pv, carry


def _sb_pair_out(even, odd, group):
    lane = lax.broadcasted_iota(jnp.int32, even.shape, 1)
    if group == 0:
        return jnp.where(lane < SB_DIM, even, pltpu.roll(odd, SB_DIM, 1))
    return jnp.where(lane < SB_DIM, pltpu.roll(even, SB_DIM, 1), odd)


def _sb_prompt_kernel(qi_ref, kb_ref, q_ref, k_ref, v_ref, u_ref, o_ref, acc_ref, carry_ref):
    pair_idx = pl.program_id(1)
    qi = qi_ref[pair_idx]
    kb = kb_ref[pair_idx]
    tq = q_ref.shape[2]
    tk = k_ref.shape[1]
    rows = SB_HEADS * tq

    def update(diagonal):
        visible = None
        if diagonal:
            tok = lax.rem(lax.broadcasted_iota(jnp.int32, (rows, tk), 0), tq)
            visible = lax.broadcasted_iota(jnp.int32, (rows, tk), 1) < tok
        z = _dot_nt(q_ref[0].reshape(rows, HEAD_PAD), k_ref[0])
        carry = jnp.zeros((rows, LANE), F32) if diagonal else carry_ref[...]
        pv, carry = _sb_block(z, visible, u_ref[...], carry, v_ref[0])
        acc_ref[...] = pv if diagonal else acc_ref[...] + pv
        carry_ref[...] = carry

    @pl.when(kb == qi)
    def _():
        update(True)

    @pl.when(kb < qi)
    def _():
        update(False)

    @pl.when(kb == 0)
    def _():
        per_group = SB_HEADS // SB_KV_HEADS
        for pair in range(SB_HEADS // 2):
            even = acc_ref[2 * pair * tq:(2 * pair + 1) * tq]
            odd = acc_ref[(2 * pair + 1) * tq:(2 * pair + 2) * tq]
            o_ref[0, :, pair * LANE:(pair + 1) * LANE] = _sb_pair_out(even, odd, (2 * pair) // per_group).astype(BF)


def _sb_prompt(q, k, v, *, blk):
    b, _, t, _ = q.shape
    qi, kb = _triangle(t // blk, descending=True)
    u = _neg_suffix_matrix(blk)
    grid_spec = pltpu.PrefetchScalarGridSpec(
        num_scalar_prefetch=2,
        grid=(b, qi.shape[0]),
        in_specs=[pl.BlockSpec((1, SB_HEADS, blk, HEAD_PAD), lambda i, p, qi, kb: (i, 0, qi[p], 0)),
                  pl.BlockSpec((1, blk, k.shape[2]), lambda i, p, qi, kb: (i, kb[p], 0)),
                  pl.BlockSpec((1, blk, v.shape[2]), lambda i, p, qi, kb: (i, kb[p], 0)),
                  pl.BlockSpec(u.shape, lambda i, p, qi, kb: (0, 0))],
        out_specs=pl.BlockSpec((1, blk, SB_HEADS * SB_DIM), lambda i, p, qi, kb: (i, qi[p], 0)),
        scratch_shapes=[pltpu.VMEM((SB_HEADS * blk, LANE), F32)] * 2)
    return pl.pallas_call(
        _sb_prompt_kernel,
        grid_spec=grid_spec,
        out_shape=jax.ShapeDtypeStruct((b, t, SB_HEADS * SB_DIM), BF),
        compiler_params=_params("arbitrary", "arbitrary"),
        name="sb_prompt",
    )(qi, kb, q, k, v, u)


def _decode_kernel(pt_ref, qlat_ref, qpe_ref, sbq_ref, latn_ref, kpen_ref, sbkn_ref, sbvn_ref, u_ref,
                   latc_ref, kpec_ref, sbkc_ref, sbvc_ref,
                   olat_ref, osb_ref,
                   lat_buf, kpe_buf, sbk_buf, sbv_buf, sems, m_ref, l_ref, acc_ref, sbacc_ref, carry_ref,
                   *, layer, n_pages, ppc):
    s_idx = pl.program_id(0)
    c_idx = pl.program_id(1)
    n_seq = pl.num_programs(0)
    n_chunks = pl.num_programs(1)
    step = s_idx * n_chunks + c_idx
    slot = lax.rem(step, 2)
    page = lat_buf.shape[2]
    rq = qlat_ref.shape[1]
    rows = rq * MLA_HEADS
    transposed = ((kpec_ref, kpe_buf), (sbkc_ref, sbk_buf), (sbvc_ref, sbv_buf))

    def copies(seq, chunk, slot_):
        base = seq * n_pages + (n_chunks - 1 - chunk) * ppc
        out = []
        for p in range(ppc):
            pg = pt_ref[base + p]
            out.append(pltpu.make_async_copy(latc_ref.at[layer, pg], lat_buf.at[slot_, p], sems.at[slot_, 0]))
            for a, (cache, buf) in enumerate(transposed):
                out.append(pltpu.make_async_copy(cache.at[layer, pg], buf.at[slot_, :, pl.ds(p * page, page)],
                                                 sems.at[slot_, a + 1]))
        return out

    @pl.when(step == 0)
    def _():
        for cp in copies(0, 0, 0):
            cp.start()

    @pl.when(step + 1 < n_seq * n_chunks)
    def _():
        last = c_idx + 1 == n_chunks
        nxt_seq = jnp.where(last, s_idx + 1, s_idx)
        nxt_chunk = jnp.where(last, 0, c_idx + 1)
        for cp in copies(nxt_seq, nxt_chunk, 1 - slot):
            cp.start()

    def head_rows(ref, width, take):
        return jnp.concatenate([ref[0, :, hd * width:hd * width + take] for hd in range(MLA_HEADS)], axis=0)

    qlat = head_rows(qlat_ref, KV_RANK, KV_RANK).astype(BF)
    qpe = head_rows(qpe_ref, HEAD_PAD, MLA_ROPE).astype(BF)
    sbq = head_rows(sbq_ref, HEAD_PAD, HEAD_PAD).astype(BF)

    def mla_update(s, kv):
        tk = s.shape[1]
        m_prev = m_ref[...]
        m_new = jnp.maximum(m_prev, jnp.max(s, axis=1, keepdims=True))
        alpha = jnp.exp(m_prev - m_new)
        p = jnp.exp(s - _lane_tile(m_new, tk // LANE))
        l_ref[...] = alpha * l_ref[...] + jnp.sum(p, axis=1, keepdims=True)
        m_ref[...] = m_new
        acc_ref[...] = _lane_tile(alpha, KV_RANK // LANE) * acc_ref[...] + _dot(p.astype(BF), kv)

    @pl.when(c_idx == 0)
    def _():
        m_ref[...] = jnp.full(m_ref.shape, -1e30, F32)
        l_ref[...] = jnp.zeros(l_ref.shape, F32)
        acc_ref[...] = jnp.zeros(acc_ref.shape, F32)
        width = u_ref.shape[1]

        def padded(ref):
            v = ref[0]
            return jnp.concatenate([v, jnp.zeros((width - rq, v.shape[1]), F32)], axis=0).astype(BF)

        tok = lax.rem(lax.broadcasted_iota(jnp.int32, (rows, width), 0), rq)
        key = lax.broadcasted_iota(jnp.int32, (rows, width), 1)
        latn = padded(latn_ref)
        s = _dot_nt(qlat, latn) + _dot_nt(qpe, padded(kpen_ref))
        mla_update(jnp.where(key <= tok, s, -1e30), latn)
        z = _dot_nt(sbq, padded(sbkn_ref))
        pv, carry = _sb_block(z, key < tok, u_ref[...], jnp.zeros((rows, LANE), F32), padded(sbvn_ref))
        sbacc_ref[...] = pv
        carry_ref[...] = carry

    for cp in copies(s_idx, c_idx, slot):
        cp.wait()

    tk = ppc * page
    latb = lat_buf[slot].reshape(tk, KV_RANK).astype(BF)
    mla_update(_dot_nt(qlat, latb) + _dot(qpe, kpe_buf[slot].astype(BF)), latb)

    pv, carry = _sb_block(_dot(sbq, sbk_buf[slot].astype(BF)), None, u_ref[...], carry_ref[...],
                          sbv_buf[slot].astype(BF), v_transposed=True)
    carry_ref[...] = carry
    sbacc_ref[...] += pv

    @pl.when(c_idx == n_chunks - 1)
    def _():
        o = acc_ref[...] / _lane_tile(l_ref[...], KV_RANK // LANE)
        olat_ref[:, 0, :, :] = o.reshape(MLA_HEADS, rq, KV_RANK)
        acc = sbacc_ref[...]
        per_group = SB_HEADS // SB_KV_HEADS
        for pair in range(SB_HEADS // 2):
            even = acc[(2 * pair) * rq:(2 * pair + 1) * rq]
            odd = acc[(2 * pair + 1) * rq:(2 * pair + 2) * rq]
            osb_ref[0, :, pair * LANE:(pair + 1) * LANE] = _sb_pair_out(even, odd, (2 * pair) // per_group)


def _decode(page_table, qlat, qpe, sbq, latn, kpen, sbkn, sbvn, caches, *, layer):
    s, rq, _ = qlat.shape
    n_pages = page_table.shape[1]
    ppc = min(PAGES_PER_CHUNK, n_pages)
    latc, kpec, sbkc, sbvc = caches
    page = latc.shape[2]
    u = _neg_suffix_matrix(2 * page)
    rows = rq * MLA_HEADS

    def seq_spec(c):
        return pl.BlockSpec((1, rq, c), lambda i, j, pt: (i, 0, 0))

    any_spec = pl.BlockSpec(memory_space=pl.ANY)
    grid_spec = pltpu.PrefetchScalarGridSpec(
        num_scalar_prefetch=1,
        grid=(s, n_pages // ppc),
        in_specs=[seq_spec(qlat.shape[2]), seq_spec(qpe.shape[2]), seq_spec(sbq.shape[2]),
                  seq_spec(latn.shape[2]), seq_spec(kpen.shape[2]), seq_spec(sbkn.shape[2]), seq_spec(sbvn.shape[2]),
                  pl.BlockSpec(u.shape, lambda i, j, pt: (0, 0)),
                  any_spec, any_spec, any_spec, any_spec],
        out_specs=[pl.BlockSpec((MLA_HEADS, 1, rq, KV_RANK), lambda i, j, pt: (0, i, 0, 0)),
                   pl.BlockSpec((1, rq, SB_HEADS * SB_DIM), lambda i, j, pt: (i, 0, 0))],
        scratch_shapes=[pltpu.VMEM((2, ppc, page, KV_RANK), F32),
                        pltpu.VMEM((2, MLA_ROPE, ppc * page), F32),
                        pltpu.VMEM((2, SB_KV_HEADS * SB_DIM, ppc * page), F32),
                        pltpu.VMEM((2, SB_KV_HEADS * SB_DIM, ppc * page), F32),
                        pltpu.SemaphoreType.DMA((2, 4)),
                        pltpu.VMEM((rows, LANE), F32),
                        pltpu.VMEM((rows, LANE), F32),
                        pltpu.VMEM((rows, KV_RANK), F32),
                        pltpu.VMEM((rows, LANE), F32),
                        pltpu.VMEM((rows, LANE), F32)])
    return pl.pallas_call(
        functools.partial(_decode_kernel, layer=layer, n_pages=n_pages, ppc=ppc),
        grid_spec=grid_spec,
        out_shape=[jax.ShapeDtypeStruct((MLA_HEADS, s, rq, KV_RANK), F32),
                   jax.ShapeDtypeStruct((s, rq, SB_HEADS * SB_DIM), F32)],
        compiler_params=_params("arbitrary", "arbitrary"),
        name="decode_attention",
    )(page_table.reshape(-1), qlat, qpe, sbq, latn, kpen, sbkn, sbvn, u, latc, kpec, sbkc, sbvc)


def _merge_kernel(*refs, absorbed):
    if absorbed:
        (a_ref, olat_ref, osb_ref, gate_ref, x_ref, mod_ref, wuv_ref,
         wco_ref, wmo_ref, wso_ref, wout_ref, o_ref) = refs
    else:
        (a_ref, omla_ref, osb_ref, gate_ref, x_ref, mod_ref,
         wco_ref, wmo_ref, wso_ref, wout_ref, o_ref) = refs
    ts, tr, d = x_ref.shape
    tm = ts * tr

    def rows2(ref):
        return ref[...].reshape(tm, ref.shape[-1])

    if absorbed:
        omla = jnp.zeros((tm, MLA_HEADS * MLA_V), F32)
        for hd in range(MLA_HEADS):
            omla = omla + _dot(olat_ref[hd].reshape(tm, KV_RANK).astype(BF), wuv_ref[hd])
        omla = omla.astype(BF)
    else:
        omla = rows2(omla_ref)
    gate = rows2(gate_ref)
    merged = (gate[:, 0:d] * _dot(rows2(a_ref), wco_ref[...])
              + gate[:, d:2 * d] * _dot(omla, wmo_ref[...])
              + gate[:, 2 * d:3 * d] * _dot(rows2(osb_ref).astype(BF), wso_ref[...]))
    mix = _dot(merged.astype(BF), wout_ref[...])
    o_ref[...] = x_ref[...] + mod_ref[...][:, 2:3, :] * mix.reshape(ts, tr, d)


def _merge(a, omla, osb, gate, x, mod, wl, *, absorbed, ts, tr):
    s, r, d = x.shape

    def row_spec(c):
        return pl.BlockSpec((ts, tr, c), lambda i, j: (i, j, 0))

    if absorbed:
        mla_spec = pl.BlockSpec((MLA_HEADS, ts, tr, KV_RANK), lambda i, j: (0, i, j, 0))
        weights = [wl["w_uv_pad"]]
    else:
        mla_spec = row_spec(omla.shape[2])
        weights = []
    weights += [wl["w_conv_out"], wl["w_mla_out"], wl["w_sb_out"], wl["w_out"]]
    return pl.pallas_call(
        functools.partial(_merge_kernel, absorbed=absorbed),
        grid=(s // ts, r // tr),
        in_specs=[row_spec(a.shape[2]), mla_spec, row_spec(osb.shape[2]), row_spec(gate.shape[2]), row_spec(d),
                  pl.BlockSpec((ts, 6, d), lambda i, j: (i, 0, 0))] + [_const_spec(w.shape) for w in weights],
        out_specs=row_spec(d),
        out_shape=jax.ShapeDtypeStruct((s, r, d), F32),
        compiler_params=_params("arbitrary", "arbitrary"),
        name="merge_sample" if absorbed else "merge_prompt",
    )(a, omla, osb, gate, x, mod, *weights)


def _ffn_kernel(*refs, final):
    (x_ref, mod_ref, ctx_ref, gn_ref, wup_ref, wdw_ref, bdw_ref, wdown_ref) = refs[:8]
    if final:
        gf_ref, o_ref, st_ref, y_ref, carry_ref, win_ref, act_ref = refs[8:]
    else:
        o_ref, st_ref, carry_ref, win_ref, act_ref = refs[8:]
    ts, tr, d = x_ref.shape
    tm = ts * tr
    pad = SUBLANE
    ctx_rows = FFN_K - 1

    @pl.when(pl.program_id(1) == 0)
    def _():
        carry_ref[...] = ctx_ref[...]

    x = x_ref[...]
    mod = mod_ref[...]
    h = _rms(x, gn_ref[...]) * (1.0 + mod[:, 4:5, :]) + mod[:, 3:4, :]
    hb = h.reshape(tm, d).astype(BF)

    def conv_chunk(c0):
        up = _dot(hb, wup_ref[:, c0:c0 + FFN_CHUNK]).reshape(ts, tr, FFN_CHUNK)
        win_ref[:, pad:pad + tr, :] = up
        win_ref[:, pad - ctx_rows:pad, :] = carry_ref[:, :, c0:c0 + FFN_CHUNK]
        y = bdw_ref[:, c0:c0 + FFN_CHUNK][None] + wdw_ref[2:3, c0:c0 + FFN_CHUNK][None] * up
        for k in range(ctx_rows):
            y = y + wdw_ref[k:k + 1, c0:c0 + FFN_CHUNK][None] * win_ref[:, pad - ctx_rows + k:pad - ctx_rows + k + tr, :]
        carry_ref[:, :, c0:c0 + FFN_CHUNK] = win_ref[:, pad + tr - ctx_rows:pad + tr, :]
        return y.reshape(tm, FFN_CHUNK)

    for i in range(D_FF // FFN_CHUNK):
        a = conv_chunk(i * FFN_CHUNK)
        g = conv_chunk(D_FF + i * FFN_CHUNK)
        act_ref[:, i * FFN_CHUNK:(i + 1) * FFN_CHUNK] = (a * jax.nn.sigmoid(a) * g).astype(BF)
    st_ref[...] = carry_ref[...]
    out = x + mod[:, 5:6, :] * _dot(act_ref[...], wdown_ref[...]).reshape(ts, tr, d)
    o_ref[...] = out
    if final:
        y_ref[...] = _rms(out, gf_ref[...])


def _ffn(x, mod, ctx, wl, g_final, *, final, ts, tr):
    s, r, d = x.shape

    def row_spec():
        return pl.BlockSpec((ts, tr, d), lambda i, j: (i, j, 0))

    state_spec = pl.BlockSpec((ts, FFN_K - 1, 2 * D_FF), lambda i, j: (i, 0, 0))
    weights = [wl["g_norm_ffn"], wl["w_ffn_up"], wl["w_ffn_dw"], wl["b_ffn_dw"], wl["w_ffn_down"]]
    if final:
        weights.append(g_final)
    out_specs = [row_spec(), state_spec]
    out_shape = [jax.ShapeDtypeStruct((s, r, d), F32), jax.ShapeDtypeStruct((s, FFN_K - 1, 2 * D_FF), F32)]
    if final:
        out_specs.append(row_spec())
        out_shape.append(jax.ShapeDtypeStruct((s, r, d), F32))
    return pl.pallas_call(
        functools.partial(_ffn_kernel, final=final),
        grid=(s // ts, r // tr),
        in_specs=[row_spec(), pl.BlockSpec((ts, 6, d), lambda i, j: (i, 0, 0)), state_spec]
        + [_const_spec(w.shape) for w in weights],
        out_specs=out_specs,
        out_shape=out_shape,
        scratch_shapes=[pltpu.VMEM((ts, FFN_K - 1, 2 * D_FF), F32),
                        pltpu.VMEM((ts, tr + SUBLANE, FFN_CHUNK), F32),
                        pltpu.VMEM((ts * tr, D_FF), BF)],
        compiler_params=_params("arbitrary", "arbitrary"),
        name="ffn_final" if final else "ffn",
    )(x, mod, ctx, *weights)


def _pad_last(w, n):
    return jnp.pad(w, [(0, 0)] * (w.ndim - 1) + [(0, n - w.shape[-1])])


def _prep_weights(w_in, w_uq, w_uk, w_uv):
    depth, d, _ = w_in.shape
    cuts = np.cumsum(IN_SIZES)[:-1]
    ua, ub, cq, ckv, kpe, sbq, sbk, sbv, gate = jnp.split(w_in, cuts, axis=-1)
    half = MLA_ROPE // 2
    rot = lambda w: jnp.concatenate([-w[..., half:], w[..., :half]], axis=-1)
    rep = LANE // MLA_ROPE
    kpe_t = jnp.tile(kpe, (1, 1, rep))
    kper_t = jnp.tile(rot(kpe), (1, 1, rep))
    sbq4 = sbq.reshape(depth, d, SB_HEADS, SB_DIM)
    per_group = SB_HEADS // SB_KV_HEADS
    in_g0 = (jnp.arange(SB_HEADS) < per_group)[None, None, :, None]
    sbq_pad = jnp.concatenate([jnp.where(in_g0, sbq4, 0.0), jnp.where(in_g0, 0.0, sbq4)], axis=-1)
    sbq_pad = sbq_pad.reshape(depth, d, SB_HEADS * HEAD_PAD)
    w_main = jnp.concatenate([ua, ub, cq, ckv, kpe_t, kper_t, sbq_pad, sbk, sbv, gate], axis=-1).astype(BF)

    wq4 = w_uq.reshape(depth, Q_RANK, MLA_HEADS, MLA_NOPE + MLA_ROPE)
    nope, pe = wq4[..., :MLA_NOPE], wq4[..., MLA_NOPE:]
    w_q = _pad_last(wq4, HEAD_PAD).reshape(depth, Q_RANK, -1).astype(BF)
    w_qr = _pad_last(jnp.concatenate([jnp.zeros_like(nope), rot(pe)], axis=-1), HEAD_PAD)
    w_qr = w_qr.reshape(depth, Q_RANK, -1).astype(BF)
    w_qpe = _pad_last(pe, HEAD_PAD).reshape(depth, Q_RANK, -1).astype(BF)
    w_qper = _pad_last(rot(pe), HEAD_PAD).reshape(depth, Q_RANK, -1).astype(BF)
    w_k = _pad_last(w_uk, HEAD_PAD).reshape(depth, KV_RANK, -1).astype(BF)
    w_v = w_uv.reshape(depth, KV_RANK, -1).astype(BF)
    w_ukt = jnp.transpose(w_uk, (0, 2, 3, 1))
    w_ukt = jnp.pad(w_ukt, ((0, 0), (0, 0), (0, HEAD_PAD - MLA_NOPE), (0, 0))).astype(BF)
    eye = jnp.eye(MLA_HEADS, dtype=w_uv.dtype)
    w_uv_pad = jnp.einsum("lrhv,hg->lhrgv", w_uv, eye).reshape(depth, MLA_HEADS, KV_RANK, MLA_HEADS * MLA_V)
    return dict(w_main=w_main, w_q=w_q, w_qr=w_qr, w_qpe=w_qpe, w_qper=w_qper, w_k=w_k, w_v=w_v,
                w_ukt=w_ukt, w_uv_pad=w_uv_pad.astype(BF))


def _rope_tables(pos):
    half = MLA_ROPE // 2
    inv_freq = ROPE_THETA ** (-jnp.arange(half, dtype=F32) / half)
    ang = pos.astype(F32)[:, None] * inv_freq[None, :]
    cos, sin = jnp.cos(ang), jnp.sin(ang)
    n = pos.shape[0]
    cos2 = jnp.concatenate([cos, cos], axis=-1)
    sin2 = jnp.concatenate([sin, sin], axis=-1)
    ones, zeros = jnp.ones((n, MLA_NOPE), F32), jnp.zeros((n, MLA_NOPE), F32)
    rep = LANE // MLA_ROPE
    return jnp.stack([
        MLA_SCALE * _pad_last(jnp.concatenate([ones, cos2], axis=-1), LANE),
        MLA_SCALE * _pad_last(jnp.concatenate([zeros, sin2], axis=-1), LANE),
        jnp.tile(cos2, (1, rep)),
        jnp.tile(sin2, (1, rep)),
        MLA_SCALE * _pad_last(cos2, LANE),
        MLA_SCALE * _pad_last(sin2, LANE)])


def kernel(x_prompt, x_sample, c_prompt, c_sample, cache_mla_latent, cache_mla_krope, cache_sb_k, cache_sb_v, state_conv, state_ffn_conv, page_table, w_ada, b_ada, g_norm_mix, g_norm_ffn, w_in, b_gate, w_dw, b_dw, g_conv_ln, b_conv_ln, w_conv_out, g_q, w_uq, g_kv, w_uk, w_uv, w_mla_out, w_sb_out, w_out, w_ffn_up, w_ffn_dw, b_ffn_dw, w_ffn_down, g_final):
    depth = w_in.shape[0]
    b, t, d = x_prompt.shape
    s, r, _ = x_sample.shape
    page = cache_mla_latent.shape[2]
    past_len = page_table.shape[1] * page
    assert r == SUBLANE and t % ATT_BLOCK == 0 and t % SB_BLOCK == 0 and t % ROW_TILE == 0
    assert s % SEQS_PER_TILE == 0 and page == LANE

    prepped = _prep_weights(w_in, w_uq, w_uk, w_uv)
    row = lambda w: w.reshape(depth, 1, w.shape[-1])
    stacked = dict(
        prepped,
        g_norm_mix=row(g_norm_mix), g_norm_ffn=row(g_norm_ffn), g_q=row(g_q), g_kv=row(g_kv),
        b_gate=row(b_gate), w_dw=w_dw, b_dw=row(b_dw), g_conv_ln=row(g_conv_ln), b_conv_ln=row(b_conv_ln),
        w_conv_out=w_conv_out.astype(BF), w_mla_out=w_mla_out.astype(BF), w_sb_out=w_sb_out.astype(BF),
        w_out=w_out.astype(BF), w_ffn_up=w_ffn_up.astype(BF), w_ffn_dw=w_ffn_dw, b_ffn_dw=row(b_ffn_dw),
        w_ffn_down=w_ffn_down.astype(BF))
    g_fin = g_final.reshape(1, d)

    tab_p = _rope_tables(jnp.arange(t, dtype=jnp.int32))
    tab_s = _rope_tables(past_len + jnp.arange(r, dtype=jnp.int32))

    n_mod = -(-(b + s) // SUBLANE) * SUBLANE
    c_all = jnp.concatenate([c_prompt, c_sample, jnp.zeros((n_mod - b - s, d), F32)], axis=0)
    mods = _ada(c_all, w_ada, b_ada)

    feat_major = lambda c: jnp.transpose(c, (0, 1, 3, 4, 2)).reshape(c.shape[:2] + (-1, c.shape[2]))
    caches = (cache_mla_latent, jnp.swapaxes(cache_mla_krope, 2, 3), feat_major(cache_sb_k), feat_major(cache_sb_v))
    ffn_zero = jnp.zeros((b, FFN_K - 1, 2 * D_FF), F32)

    xp, xs = x_prompt, x_sample
    yp = ys = None
    st_p, st_s = [], []
    for l in range(depth):
        wl = {k: v[l] for k, v in stacked.items()}
        final = l == depth - 1
        mod_p = mods[l, :b].reshape(b, 6, d)
        mod_s = mods[l, b:b + s].reshape(s, 6, d)

        (u, q, lat, kpe, sbq, sbk, sbv, gate, k, v, sbkb, sbvb) = _inproj(
            xp, mod_p, tab_p, wl, prompt=True, ts=1, tr=ROW_TILE)
        a = _conv_prompt(u, wl, tm=ROW_TILE)
        omla = _mla_prompt(q, k, v, blk=ATT_BLOCK)
        osb = _sb_prompt(sbq, sbkb, sbvb, blk=SB_BLOCK)
        x1 = _merge(a, omla, osb, gate, xp, mod_p, wl, absorbed=False, ts=1, tr=ROW_TILE)
        res = _ffn(x1, mod_p, ffn_zero, wl, g_fin, final=final, ts=1, tr=ROW_TILE)
        xp, ffn_state = res[0], res[1]
        if final:
            yp = res[2]
        st_p.append((lat, kpe, sbk.reshape(b, t, SB_KV_HEADS, SB_DIM), sbv.reshape(b, t, SB_KV_HEADS, SB_DIM),
                     u[:, t - (CONV_K - 1):], ffn_state))

        (u, lat, kpe, sbq, sbk, sbv, gate, qlat, qpe) = _inproj(
            xs, mod_s, tab_s, wl, prompt=False, ts=SEQS_PER_TILE, tr=r)
        full = jnp.concatenate([state_conv[l], u], axis=1)
        a = _conv_sample(full, wl, ts=SEQS_PER_TILE, tr=r)
        olat, osb = _decode(page_table, qlat, qpe, sbq, lat, kpe, sbk, sbv, caches, layer=l)
        x1 = _merge(a, olat, osb, gate, xs, mod_s, wl, absorbed=True, ts=SEQS_PER_TILE, tr=r)
        res = _ffn(x1, mod_s, state_ffn_conv[l], wl, g_fin, final=final, ts=SEQS_PER_TILE, tr=r)
        xs, ffn_state = res[0], res[1]
        if final:
            ys = res[2]
        st_s.append((lat, kpe, sbk.reshape(s, r, SB_KV_HEADS, SB_DIM), sbv.reshape(s, r, SB_KV_HEADS, SB_DIM),
                     full[:, r:], ffn_state))

    stack = lambda sts, i: jnp.stack([st[i] for st in sts], axis=0)
    return (yp, ys,
            stack(st_p, 0), stack(st_p, 1), stack(st_p, 2), stack(st_p, 3), stack(st_p, 4), stack(st_p, 5),
            stack(st_s, 0), stack(st_s, 1), stack(st_s, 2), stack(st_s, 3), stack(st_s, 4), stack(st_s, 5))
```

```python
import functools

import jax
import jax.numpy as jnp
import numpy as np
from jax import lax
from jax.experimental import pallas as pl
from jax.experimental.pallas import tpu as pltpu

F32 = jnp.float32
BF = jnp.bfloat16

D_MODEL = 1024
D_CONV = 512
CONV_K = 31
MLA_HEADS = 8
MLA_NOPE = 64
MLA_ROPE = 32
MLA_V = 64
Q_RANK = 384
KV_RANK = 256
ROPE_THETA = 10000.0
MLA_SCALE = (MLA_NOPE + MLA_ROPE) ** -0.5
SB_HEADS = 8
SB_KV_HEADS = 2
SB_DIM = 64
SB_SCALE = SB_DIM ** -0.5
LOG2E = 1.4426950408889634
D_FF = 2816
FFN_K = 3
N_BRANCH = 3
EPS = 1e-6
IN_SIZES = (D_CONV, D_CONV, Q_RANK, KV_RANK, MLA_ROPE, SB_HEADS * SB_DIM,
            SB_KV_HEADS * SB_DIM, SB_KV_HEADS * SB_DIM, N_BRANCH * D_MODEL)

LANE = 128
SUBLANE = 8
VMEM_LIMIT = 56 * 1024 * 1024

HEAD_PAD = LANE
C_UA = 0
C_UB = C_UA + D_CONV
C_CQ = C_UB + D_CONV
C_CKV = C_CQ + Q_RANK
C_KPE = C_CKV + KV_RANK
C_KPER = C_KPE + LANE
C_SBQ = C_KPER + LANE
C_SBK = C_SBQ + SB_HEADS * HEAD_PAD
C_SBV = C_SBK + SB_KV_HEADS * SB_DIM
C_GATE = C_SBV + SB_KV_HEADS * SB_DIM
C_END = C_GATE + N_BRANCH * D_MODEL
GATE_CHUNK = 512

ROW_TILE = 256
ATT_BLOCK = 512
SB_BLOCK = 256
CONV_ROWS = 64
FFN_CHUNK = 256
PAGES_PER_CHUNK = 32
SEQS_PER_TILE = ROW_TILE // SUBLANE


def _params(*sem):
    return pltpu.CompilerParams(dimension_semantics=sem, vmem_limit_bytes=VMEM_LIMIT)


def _const_spec(shape):
    nd = len(shape)
    return pl.BlockSpec(shape, lambda *_: (0,) * nd)


def _rms(v, g):
    return v * lax.rsqrt(jnp.mean(v * v, axis=-1, keepdims=True) + EPS) * g


def _dot(a, b):
    return jnp.dot(a, b, preferred_element_type=F32)


def _dot_nt(a, b):
    return lax.dot_general(a, b, (((1,), (1,)), ((), ())), preferred_element_type=F32)


def _lane_tile(v, n):
    return v if n == 1 else jnp.concatenate([v] * n, axis=-1)


def _ada_kernel(c_ref, w_ref, b_ref, o_ref):
    c = c_ref[...]
    a = (c * jax.nn.sigmoid(c)).astype(BF)
    o_ref[0] = _dot(a, w_ref[0].astype(BF)) + b_ref[0]


def _ada(c_all, w_ada, b_ada):
    depth, d, n = w_ada.shape
    rows = c_all.shape[0]
    tn = 512
    return pl.pallas_call(
        _ada_kernel,
        grid=(depth, n // tn),
        in_specs=[pl.BlockSpec((rows, d), lambda l, j: (0, 0)),
                  pl.BlockSpec((1, d, tn), lambda l, j: (l, 0, j)),
                  pl.BlockSpec((1, 1, tn), lambda l, j: (l, 0, j))],
        out_specs=pl.BlockSpec((1, rows, tn), lambda l, j: (l, 0, j)),
        out_shape=jax.ShapeDtypeStruct((depth, rows, n), F32),
        compiler_params=_params("arbitrary", "arbitrary"),
        name="ada",
    )(c_all, w_ada, b_ada.reshape(depth, 1, n))


def _inproj_kernel(*refs, prompt):
    (x_ref, mod_ref, tab_ref, gn_ref, wmain_ref, gq_ref, wq_ref, wqr_ref, gkv_ref, bgate_ref) = refs[:10]
    if prompt:
        wk_ref, wv_ref = refs[10:12]
        (u_ref, q_ref, lat_ref, kpe_ref, sbq_ref, sbk_ref, sbv_ref, gate_ref,
         k_ref, v_ref, sbkb_ref, sbvb_ref) = refs[12:]
    else:
        wukt_ref, wqpe_ref, wqper_ref = refs[10:13]
        (u_ref, lat_ref, kpe_ref, sbq_ref, sbk_ref, sbv_ref, gate_ref,
         qlat_ref, qpe_ref) = refs[13:]
    ts, tr, d = x_ref.shape
    tm = ts * tr

    def rows3(v):
        return v.reshape(ts, tr, v.shape[-1])

    mod = mod_ref[...]
    h = _rms(x_ref[...], gn_ref[...]) * (1.0 + mod[:, 1:2, :]) + mod[:, 0:1, :]
    hb = h.reshape(tm, d).astype(BF)

    def seg(a, b):
        return _dot(hb, wmain_ref[:, a:b])

    tab = tab_ref[...]

    u_ref[...] = rows3(seg(C_UA, C_UB) * jax.nn.sigmoid(seg(C_UB, C_CQ)))

    cqn = _rms(seg(C_CQ, C_CKV), gq_ref[...]).astype(BF)
    qf = (rows3(_dot(cqn, wq_ref[...])) * _lane_tile(tab[0], MLA_HEADS)[None]
          + rows3(_dot(cqn, wqr_ref[...])) * _lane_tile(tab[1], MLA_HEADS)[None])
    if prompt:
        q_ref[...] = qf.astype(BF)

    lat = _rms(seg(C_CKV, C_KPE), gkv_ref[...])
    lat_ref[...] = rows3(lat)
    kr = rows3(seg(C_KPE, C_KPER)) * tab[2][None] + rows3(seg(C_KPER, C_SBQ)) * tab[3][None]
    kpe_ref[...] = kr[:, :, :MLA_ROPE]

    if prompt:
        latb = lat.astype(BF)
        lane = lax.broadcasted_iota(jnp.int32, (1, 1, LANE), 2)
        kr_head = jnp.where((lane >= MLA_NOPE) & (lane < MLA_NOPE + MLA_ROPE), kr, 0.0)
        k_ref[...] = (rows3(_dot(latb, wk_ref[...])) + _lane_tile(kr_head, MLA_HEADS)).astype(BF)
        v_ref[...] = rows3(_dot(latb, wv_ref[...])).astype(BF)
    else:
        qfb = qf.reshape(tm, MLA_HEADS * HEAD_PAD).astype(BF)
        for hd in range(MLA_HEADS):
            ql = _dot(qfb[:, hd * HEAD_PAD:(hd + 1) * HEAD_PAD], wukt_ref[hd])
            qlat_ref[:, :, hd * KV_RANK:(hd + 1) * KV_RANK] = rows3(ql)
        qpe_ref[...] = (rows3(_dot(cqn, wqpe_ref[...])) * _lane_tile(tab[4], MLA_HEADS)[None]
                        + rows3(_dot(cqn, wqper_ref[...])) * _lane_tile(tab[5], MLA_HEADS)[None])

    sbq = seg(C_SBQ, C_SBK) * (SB_SCALE * LOG2E)
    if prompt:
        for hd in range(SB_HEADS):
            sbq_ref[0, hd] = sbq[:, hd * HEAD_PAD:(hd + 1) * HEAD_PAD].astype(BF)
    else:
        sbq_ref[...] = rows3(sbq)
    sbk = rows3(seg(C_SBK, C_SBV))
    sbv = rows3(seg(C_SBV, C_GATE))
    sbk_ref[...] = sbk
    sbv_ref[...] = sbv
    if prompt:
        sbkb_ref[...] = sbk.astype(BF)
        sbvb_ref[...] = sbv.astype(BF)

    for c in range(0, N_BRANCH * D_MODEL, GATE_CHUNK):
        g = jax.nn.sigmoid(seg(C_GATE + c, C_GATE + c + GATE_CHUNK) + bgate_ref[:, c:c + GATE_CHUNK])
        gate_ref[:, :, c:c + GATE_CHUNK] = rows3(g).astype(BF)


def _inproj(x, mod, tab, wl, *, prompt, ts, tr):
    s, r, d = x.shape
    grid = (s // ts, r // tr)

    def row_spec(c):
        return pl.BlockSpec((ts, tr, c), lambda i, j: (i, j, 0))

    def row_shape(c, dt):
        return jax.ShapeDtypeStruct((s, r, c), dt)

    weights = [wl["g_norm_mix"], wl["w_main"], wl["g_q"], wl["w_q"], wl["w_qr"], wl["g_kv"], wl["b_gate"]]
    if prompt:
        weights += [wl["w_k"], wl["w_v"]]
    else:
        weights += [wl["w_ukt"], wl["w_qpe"], wl["w_qper"]]
    in_specs = [row_spec(d),
                pl.BlockSpec((ts, 6, d), lambda i, j: (i, 0, 0)),
                pl.BlockSpec((6, tr, LANE), lambda i, j: (0, j, 0))]
    in_specs += [_const_spec(w.shape) for w in weights]
    outs = [(D_CONV, F32)] + ([(MLA_HEADS * HEAD_PAD, BF)] if prompt else [])
    outs += [(KV_RANK, F32), (MLA_ROPE, F32), (SB_HEADS * HEAD_PAD, BF if prompt else F32),
             (SB_KV_HEADS * SB_DIM, F32), (SB_KV_HEADS * SB_DIM, F32), (N_BRANCH * D_MODEL, BF)]
    if prompt:
        outs += [(MLA_HEADS * HEAD_PAD, BF), (MLA_HEADS * MLA_V, BF),
                 (SB_KV_HEADS * SB_DIM, BF), (SB_KV_HEADS * SB_DIM, BF)]
    else:
        outs += [(MLA_HEADS * KV_RANK, F32), (MLA_HEADS * HEAD_PAD, F32)]
    out_specs = [row_spec(c) for c, _ in outs]
    out_shape = [row_shape(c, dt) for c, dt in outs]
    if prompt:
        assert ts == 1
        sbq_at = 4
        out_specs[sbq_at] = pl.BlockSpec((1, SB_HEADS, tr, HEAD_PAD), lambda i, j: (i, 0, j, 0))
        out_shape[sbq_at] = jax.ShapeDtypeStruct((s, SB_HEADS, r, HEAD_PAD), BF)
    return pl.pallas_call(
        functools.partial(_inproj_kernel, prompt=prompt),
        grid=grid,
        in_specs=in_specs,
        out_specs=out_specs,
        out_shape=out_shape,
        compiler_params=_params("arbitrary", "arbitrary"),
        name="inproj_prompt" if prompt else "inproj_sample",
    )(x, mod, tab, *weights)


def _ln_swish(acc, g, b):
    xc = acc - jnp.mean(acc, axis=-1, keepdims=True)
    y = xc * lax.rsqrt(jnp.mean(xc * xc, axis=-1, keepdims=True) + EPS) * g + b
    return y * jax.nn.sigmoid(y)


def _conv_prompt_kernel(cur_ref, prev_ref, w_ref, b_ref, g_ref, bl_ref, o_ref, win_ref):
    tm = cur_ref.shape[1]
    halo = prev_ref.shape[1]
    j = pl.program_id(1)
    win_ref[0, 0:halo, :] = jnp.where(j > 0, prev_ref[0], 0.0)
    win_ref[0, halo:, :] = cur_ref[0]
    n = tm + halo - SUBLANE
    for sft in range(1, SUBLANE):
        win_ref[sft, 0:n, :] = win_ref[0, sft:sft + n, :]
    off = halo - (CONV_K - 1)
    for r0 in range(0, tm, CONV_ROWS):
        acc = jnp.broadcast_to(b_ref[...], (CONV_ROWS, D_CONV))
        for k in range(CONV_K):
            sft = (off + k) % SUBLANE
            base = off + k - sft + r0
            acc = acc + w_ref[k:k + 1, :] * win_ref[sft, base:base + CONV_ROWS, :]
        o_ref[0, r0:r0 + CONV_ROWS, :] = _ln_swish(acc, g_ref[...], bl_ref[...]).astype(BF)


def _conv_prompt(u, wl, *, tm):
    b, t, c = u.shape
    halo = 32
    ratio = tm // halo
    return pl.pallas_call(
        _conv_prompt_kernel,
        grid=(b, t // tm),
        in_specs=[pl.BlockSpec((1, tm, c), lambda i, j: (i, j, 0)),
                  pl.BlockSpec((1, halo, c), lambda i, j: (i, jnp.maximum(j * ratio - 1, 0), 0)),
                  _const_spec(wl["w_dw"].shape), _const_spec(wl["b_dw"].shape),
                  _const_spec(wl["g_conv_ln"].shape), _const_spec(wl["b_conv_ln"].shape)],
        out_specs=pl.BlockSpec((1, tm, c), lambda i, j: (i, j, 0)),
        out_shape=jax.ShapeDtypeStruct((b, t, c), BF),
        scratch_shapes=[pltpu.VMEM((SUBLANE, tm + halo, c), F32)],
        compiler_params=_params("arbitrary", "arbitrary"),
        name="conv_prompt",
    )(u, u, wl["w_dw"], wl["b_dw"], wl["g_conv_ln"], wl["b_conv_ln"])


def _conv_sample_kernel(full_ref, w_ref, b_ref, g_ref, bl_ref, o_ref):
    ts, _, c = full_ref.shape
    tr = o_ref.shape[1]
    acc = jnp.broadcast_to(b_ref[...][None], (ts, tr, c))
    for k in range(CONV_K):
        acc = acc + w_ref[k:k + 1, :][None] * full_ref[:, k:k + tr, :]
    o_ref[...] = _ln_swish(acc, g_ref[...][None], bl_ref[...][None]).astype(BF)


def _conv_sample(full, wl, *, ts, tr):
    s, rows, c = full.shape
    return pl.pallas_call(
        _conv_sample_kernel,
        grid=(s // ts,),
        in_specs=[pl.BlockSpec((ts, rows, c), lambda i: (i, 0, 0)),
                  _const_spec(wl["w_dw"].shape), _const_spec(wl["b_dw"].shape),
                  _const_spec(wl["g_conv_ln"].shape), _const_spec(wl["b_conv_ln"].shape)],
        out_specs=pl.BlockSpec((ts, tr, c), lambda i: (i, 0, 0)),
        out_shape=jax.ShapeDtypeStruct((s, tr, c), BF),
        compiler_params=_params("arbitrary"),
        name="conv_sample",
    )(full, wl["w_dw"], wl["b_dw"], wl["g_conv_ln"], wl["b_conv_ln"])


def _triangle(n, descending):
    qi = np.concatenate([np.full(i + 1, i) for i in range(n)])
    kb = np.concatenate([np.arange(i, -1, -1) if descending else np.arange(i + 1) for i in range(n)])
    return jnp.asarray(qi, jnp.int32), jnp.asarray(kb, jnp.int32)


def _mla_prompt_kernel(qi_ref, kb_ref, q_ref, k_ref, v_ref, o_ref, m_ref, l_ref, acc_ref):
    pair_idx = pl.program_id(1)
    qi = qi_ref[pair_idx]
    kb = kb_ref[pair_idx]
    tq = q_ref.shape[1]
    tk = k_ref.shape[1]

    @pl.when(kb == 0)
    def _():
        m_ref[...] = jnp.full(m_ref.shape, -1e30, F32)
        l_ref[...] = jnp.zeros(l_ref.shape, F32)
        acc_ref[...] = jnp.zeros(acc_ref.shape, F32)

    def update(diagonal):
        if diagonal:
            visible = (lax.broadcasted_iota(jnp.int32, (tq, tk), 1)
                       <= lax.broadcasted_iota(jnp.int32, (tq, tk), 0))
        for hd in range(MLA_HEADS):
            s = _dot_nt(q_ref[0, :, hd * HEAD_PAD:(hd + 1) * HEAD_PAD],
                        k_ref[0, :, hd * HEAD_PAD:(hd + 1) * HEAD_PAD])
            if diagonal:
                s = jnp.where(visible, s, -1e30)
            m_prev = m_ref[hd]
            m_new = jnp.maximum(m_prev, jnp.max(s, axis=1, keepdims=True))
            alpha = jnp.exp(m_prev - m_new)
            p = jnp.exp(s - _lane_tile(m_new, tk // LANE))
            l_ref[hd] = alpha * l_ref[hd] + jnp.sum(p, axis=1, keepdims=True)
            m_ref[hd] = m_new
            pair = hd // 2
            acc_ref[hd] = alpha * acc_ref[hd] + _dot(p.astype(BF), v_ref[0, :, pair * LANE:(pair + 1) * LANE])

    @pl.when(kb < qi)
    def _():
        update(False)

    @pl.when(kb == qi)
    def _():
        update(True)
        lane = lax.broadcasted_iota(jnp.int32, (tq, LANE), 1)
        for pair in range(MLA_HEADS // 2):
            even = acc_ref[2 * pair] / l_ref[2 * pair]
            odd = acc_ref[2 * pair + 1] / l_ref[2 * pair + 1]
            o_ref[0, :, pair * LANE:(pair + 1) * LANE] = jnp.where(lane < MLA_V, even, odd).astype(BF)


def _mla_prompt(q, k, v, *, blk):
    b, t, _ = q.shape
    qi, kb = _triangle(t // blk, descending=False)
    grid_spec = pltpu.PrefetchScalarGridSpec(
        num_scalar_prefetch=2,
        grid=(b, qi.shape[0]),
        in_specs=[pl.BlockSpec((1, blk, q.shape[2]), lambda i, p, qi, kb: (i, qi[p], 0)),
                  pl.BlockSpec((1, blk, k.shape[2]), lambda i, p, qi, kb: (i, kb[p], 0)),
                  pl.BlockSpec((1, blk, v.shape[2]), lambda i, p, qi, kb: (i, kb[p], 0))],
        out_specs=pl.BlockSpec((1, blk, v.shape[2]), lambda i, p, qi, kb: (i, qi[p], 0)),
        scratch_shapes=[pltpu.VMEM((MLA_HEADS, blk, LANE), F32)] * 3)
    return pl.pallas_call(
        _mla_prompt_kernel,
        grid_spec=grid_spec,
        out_shape=jax.ShapeDtypeStruct((b, t, v.shape[2]), BF),
        compiler_params=_params("arbitrary", "arbitrary"),
        name="mla_prompt",
    )(qi, kb, q, k, v)


def _neg_suffix_matrix(n):
    s = np.arange(n)[:, None]
    j = np.arange(n)[None, :]
    u = -(s > j).astype(np.float32)
    return jnp.asarray(np.concatenate([u, u], axis=0), dtype=BF)


def _sb_block(z, visible, uneg, carry, v, v_transposed=False):
    tq, tk = z.shape
    sub = uneg.shape[1]
    nb = tk // sub
    sp = jnp.maximum(z, 0.0) + jnp.log2(1.0 + jnp.exp2(-jnp.abs(z)))
    log_beta = z - sp
    if visible is not None:
        sp = jnp.where(visible, sp, 0.0)
    hi = sp.astype(BF)
    lo = (sp - hi.astype(F32)).astype(BF)
    cols = lambda x, i: x if nb == 1 else x[:, i * sub:(i + 1) * sub]
    split = [jnp.concatenate([cols(hi, i), cols(lo, i)], axis=1) for i in range(nb)]
    r = _dot(split[0] if nb == 1 else jnp.concatenate(split, axis=0), uneg)
    inner = [r[i * tq:(i + 1) * tq] for i in range(nb)]
    pieces = [None] * nb
    for i in reversed(range(nb)):
        pieces[i] = inner[i] + _lane_tile(carry, sub // LANE)
        carry = carry - jnp.sum(cols(sp, i), axis=1, keepdims=True)
    log_after = pieces[0] if nb == 1 else jnp.concatenate(pieces, axis=1)
    a = jnp.exp2(log_beta + log_after)
    if visible is not None:
        a = jnp.where(visible, a, 0.0)
    pv = _dot_nt(a.astype(BF), v) if v_transposed else _dot(a.astype(BF), v)
    return pv, carry


def _sb_pair_out(even, odd, group):
    lane = lax.broadcasted_iota(jnp.int32, even.shape, 1)
    if group == 0:
        return jnp.where(lane < SB_DIM, even, pltpu.roll(odd, SB_DIM, 1))
    return jnp.where(lane < SB_DIM, pltpu.roll(even, SB_DIM, 1), odd)


def _sb_prompt_kernel(qi_ref, kb_ref, q_ref, k_ref, v_ref, u_ref, o_ref, acc_ref, carry_ref):
    pair_idx = pl.program_id(1)
    qi = qi_ref[pair_idx]
    kb = kb_ref[pair_idx]
    tq = q_ref.shape[2]
    tk = k_ref.shape[1]
    rows = SB_HEADS * tq

    def update(diagonal):
        visible = None
        if diagonal:
            tok = lax.rem(lax.broadcasted_iota(jnp.int32, (rows, tk), 0), tq)
            visible = lax.broadcasted_iota(jnp.int32, (rows, tk), 1) < tok
        z = _dot_nt(q_ref[0].reshape(rows, HEAD_PAD), k_ref[0])
        carry = jnp.zeros((rows, LANE), F32) if diagonal else carry_ref[...]
        pv, carry = _sb_block(z, visible, u_ref[...], carry, v_ref[0])
        acc_ref[...] = pv if diagonal else acc_ref[...] + pv
        carry_ref[...] = carry

    @pl.when(kb == qi)
    def _():
        update(True)

    @pl.when(kb < qi)
    def _():
        update(False)

    @pl.when(kb == 0)
    def _():
        per_group = SB_HEADS // SB_KV_HEADS
        for pair in range(SB_HEADS // 2):
            even = acc_ref[2 * pair * tq:(2 * pair + 1) * tq]
            odd = acc_ref[(2 * pair + 1) * tq:(2 * pair + 2) * tq]
            o_ref[0, :, pair * LANE:(pair + 1) * LANE] = _sb_pair_out(even, odd, (2 * pair) // per_group).astype(BF)


def _sb_prompt(q, k, v, *, blk):
    b, _, t, _ = q.shape
    qi, kb = _triangle(t // blk, descending=True)
    u = _neg_suffix_matrix(blk)
    grid_spec = pltpu.PrefetchScalarGridSpec(
        num_scalar_prefetch=2,
        grid=(b, qi.shape[0]),
        in_specs=[pl.BlockSpec((1, SB_HEADS, blk, HEAD_PAD), lambda i, p, qi, kb: (i, 0, qi[p], 0)),
                  pl.BlockSpec((1, blk, k.shape[2]), lambda i, p, qi, kb: (i, kb[p], 0)),
                  pl.BlockSpec((1, blk, v.shape[2]), lambda i, p, qi, kb: (i, kb[p], 0)),
                  pl.BlockSpec(u.shape, lambda i, p, qi, kb: (0, 0))],
        out_specs=pl.BlockSpec((1, blk, SB_HEADS * SB_DIM), lambda i, p, qi, kb: (i, qi[p], 0)),
        scratch_shapes=[pltpu.VMEM((SB_HEADS * blk, LANE), F32)] * 2)
    return pl.pallas_call(
        _sb_prompt_kernel,
        grid_spec=grid_spec,
        out_shape=jax.ShapeDtypeStruct((b, t, SB_HEADS * SB_DIM), BF),
        compiler_params=_params("arbitrary", "arbitrary"),
        name="sb_prompt",
    )(qi, kb, q, k, v, u)


def _decode_kernel(pt_ref, qlat_ref, qpe_ref, sbq_ref, latn_ref, kpen_ref, sbkn_ref, sbvn_ref, u_ref,
                   latc_ref, kpec_ref, sbkc_ref, sbvc_ref,
                   olat_ref, osb_ref,
                   lat_buf, kpe_buf, sbk_buf, sbv_buf, sems, m_ref, l_ref, acc_ref, sbacc_ref, carry_ref,
                   *, layer, n_pages, ppc):
    s_idx = pl.program_id(0)
    c_idx = pl.program_id(1)
    n_seq = pl.num_programs(0)
    n_chunks = pl.num_programs(1)
    step = s_idx * n_chunks + c_idx
    slot = lax.rem(step, 2)
    page = lat_buf.shape[2]
    rq = qlat_ref.shape[1]
    rows = rq * MLA_HEADS
    transposed = ((kpec_ref, kpe_buf), (sbkc_ref, sbk_buf), (sbvc_ref, sbv_buf))

    def copies(seq, chunk, slot_):
        base = seq * n_pages + (n_chunks - 1 - chunk) * ppc
        out = []
        for p in range(ppc):
            pg = pt_ref[base + p]
            out.append(pltpu.make_async_copy(latc_ref.at[layer, pg], lat_buf.at[slot_, p], sems.at[slot_, 0]))
            for a, (cache, buf) in enumerate(transposed):
                out.append(pltpu.make_async_copy(cache.at[layer, pg], buf.at[slot_, :, pl.ds(p * page, page)],
                                                 sems.at[slot_, a + 1]))
        return out

    @pl.when(step == 0)
    def _():
        for cp in copies(0, 0, 0):
            cp.start()

    @pl.when(step + 1 < n_seq * n_chunks)
    def _():
        last = c_idx + 1 == n_chunks
        nxt_seq = jnp.where(last, s_idx + 1, s_idx)
        nxt_chunk = jnp.where(last, 0, c_idx + 1)
        for cp in copies(nxt_seq, nxt_chunk, 1 - slot):
            cp.start()

    def head_rows(ref, width, take):
        return jnp.concatenate([ref[0, :, hd * width:hd * width + take] for hd in range(MLA_HEADS)], axis=0)

    qlat = head_rows(qlat_ref, KV_RANK, KV_RANK).astype(BF)
    qpe = head_rows(qpe_ref, HEAD_PAD, MLA_ROPE).astype(BF)
    sbq = head_rows(sbq_ref, HEAD_PAD, HEAD_PAD).astype(BF)

    def mla_update(s, kv):
        tk = s.shape[1]
        m_prev = m_ref[...]
        m_new = jnp.maximum(m_prev, jnp.max(s, axis=1, keepdims=True))
        alpha = jnp.exp(m_prev - m_new)
        p = jnp.exp(s - _lane_tile(m_new, tk // LANE))
        l_ref[...] = alpha * l_ref[...] + jnp.sum(p, axis=1, keepdims=True)
        m_ref[...] = m_new
        acc_ref[...] = _lane_tile(alpha, KV_RANK // LANE) * acc_ref[...] + _dot(p.astype(BF), kv)

    @pl.when(c_idx == 0)
    def _():
        m_ref[...] = jnp.full(m_ref.shape, -1e30, F32)
        l_ref[...] = jnp.zeros(l_ref.shape, F32)
        acc_ref[...] = jnp.zeros(acc_ref.shape, F32)
        width = u_ref.shape[1]

        def padded(ref):
            v = ref[0]
            return jnp.concatenate([v, jnp.zeros((width - rq, v.shape[1]), F32)], axis=0).astype(BF)

        tok = lax.rem(lax.broadcasted_iota(jnp.int32, (rows, width), 0), rq)
        key = lax.broadcasted_iota(jnp.int32, (rows, width), 1)
        latn = padded(latn_ref)
        s = _dot_nt(qlat, latn) + _dot_nt(qpe, padded(kpen_ref))
        mla_update(jnp.where(key <= tok, s, -1e30), latn)
        z = _dot_nt(sbq, padded(sbkn_ref))
        pv, carry = _sb_block(z, key < tok, u_ref[...], jnp.zeros((rows, LANE), F32), padded(sbvn_ref))
        sbacc_ref[...] = pv
        carry_ref[...] = carry

    for cp in copies(s_idx, c_idx, slot):
        cp.wait()

    tk = ppc * page
    latb = lat_buf[slot].reshape(tk, KV_RANK).astype(BF)
    s = _dot_nt(qlat, latb) + _dot(qpe, kpe_buf[slot].astype(BF))
    z = _dot(sbq, sbk_buf[slot].astype(BF))
    mla_update(s, latb)
    pv, carry = _sb_block(z, None, u_ref[...], carry_ref[...], sbv_buf[slot].astype(BF), v_transposed=True)
    carry_ref[...] = carry
    sbacc_ref[...] += pv

    @pl.when(c_idx == n_chunks - 1)
    def _():
        o = acc_ref[...] / _lane_tile(l_ref[...], KV_RANK // LANE)
        olat_ref[:, 0, :, :] = o.reshape(MLA_HEADS, rq, KV_RANK)
        acc = sbacc_ref[...]
        per_group = SB_HEADS // SB_KV_HEADS
        for pair in range(SB_HEADS // 2):
            even = acc[(2 * pair) * rq:(2 * pair + 1) * rq]
            odd = acc[(2 * pair + 1) * rq:(2 * pair + 2) * rq]
            osb_ref[0, :, pair * LANE:(pair + 1) * LANE] = _sb_pair_out(even, odd, (2 * pair) // per_group)


def _decode(page_table, qlat, qpe, sbq, latn, kpen, sbkn, sbvn, caches, *, layer):
    s, rq, _ = qlat.shape
    n_pages = page_table.shape[1]
    ppc = min(PAGES_PER_CHUNK, n_pages)
    latc, kpec, sbkc, sbvc = caches
    page = latc.shape[2]
    u = _neg_suffix_matrix(2 * page)
    rows = rq * MLA_HEADS

    def seq_spec(c):
        return pl.BlockSpec((1, rq, c), lambda i, j, pt: (i, 0, 0))

    any_spec = pl.BlockSpec(memory_space=pl.ANY)
    grid_spec = pltpu.PrefetchScalarGridSpec(
        num_scalar_prefetch=1,
        grid=(s, n_pages // ppc),
        in_specs=[seq_spec(qlat.shape[2]), seq_spec(qpe.shape[2]), seq_spec(sbq.shape[2]),
                  seq_spec(latn.shape[2]), seq_spec(kpen.shape[2]), seq_spec(sbkn.shape[2]), seq_spec(sbvn.shape[2]),
                  pl.BlockSpec(u.shape, lambda i, j, pt: (0, 0)),
                  any_spec, any_spec, any_spec, any_spec],
        out_specs=[pl.BlockSpec((MLA_HEADS, 1, rq, KV_RANK), lambda i, j, pt: (0, i, 0, 0)),
                   pl.BlockSpec((1, rq, SB_HEADS * SB_DIM), lambda i, j, pt: (i, 0, 0))],
        scratch_shapes=[pltpu.VMEM((2, ppc, page, KV_RANK), F32),
                        pltpu.VMEM((2, MLA_ROPE, ppc * page), F32),
                        pltpu.VMEM((2, SB_KV_HEADS * SB_DIM, ppc * page), F32),
                        pltpu.VMEM((2, SB_KV_HEADS * SB_DIM, ppc * page), F32),
                        pltpu.SemaphoreType.DMA((2, 4)),
                        pltpu.VMEM((rows, LANE), F32),
                        pltpu.VMEM((rows, LANE), F32),
                        pltpu.VMEM((rows, KV_RANK), F32),
                        pltpu.VMEM((rows, LANE), F32),
                        pltpu.VMEM((rows, LANE), F32)])
    return pl.pallas_call(
        functools.partial(_decode_kernel, layer=layer, n_pages=n_pages, ppc=ppc),
        grid_spec=grid_spec,
        out_shape=[jax.ShapeDtypeStruct((MLA_HEADS, s, rq, KV_RANK), F32),
                   jax.ShapeDtypeStruct((s, rq, SB_HEADS * SB_DIM), F32)],
        compiler_params=_params("arbitrary", "arbitrary"),
        name="decode_attention",
    )(page_table.reshape(-1), qlat, qpe, sbq, latn, kpen, sbkn, sbvn, u, latc, kpec, sbkc, sbvc)


def _merge_kernel(*refs, absorbed):
    if absorbed:
        (a_ref, olat_ref, osb_ref, gate_ref, x_ref, mod_ref, wuv_ref,
         wco_ref, wmo_ref, wso_ref, wout_ref, o_ref) = refs
    else:
        (a_ref, omla_ref, osb_ref, gate_ref, x_ref, mod_ref,
         wco_ref, wmo_ref, wso_ref, wout_ref, o_ref) = refs
    ts, tr, d = x_ref.shape
    tm = ts * tr

    def rows2(ref):
        return ref[...].reshape(tm, ref.shape[-1])

    if absorbed:
        omla = jnp.zeros((tm, MLA_HEADS * MLA_V), F32)
        for hd in range(MLA_HEADS):
            omla = omla + _dot(olat_ref[hd].reshape(tm, KV_RANK).astype(BF), wuv_ref[hd])
        omla = omla.astype(BF)
    else:
        omla = rows2(omla_ref)
    gate = rows2(gate_ref)
    merged = (gate[:, 0:d] * _dot(rows2(a_ref), wco_ref[...])
              + gate[:, d:2 * d] * _dot(omla, wmo_ref[...])
              + gate[:, 2 * d:3 * d] * _dot(rows2(osb_ref).astype(BF), wso_ref[...]))
    mix = _dot(merged.astype(BF), wout_ref[...])
    o_ref[...] = x_ref[...] + mod_ref[...][:, 2:3, :] * mix.reshape(ts, tr, d)


def _merge(a, omla, osb, gate, x, mod, wl, *, absorbed, ts, tr):
    s, r, d = x.shape

    def row_spec(c):
        return pl.BlockSpec((ts, tr, c), lambda i, j: (i, j, 0))

    if absorbed:
        mla_spec = pl.BlockSpec((MLA_HEADS, ts, tr, KV_RANK), lambda i, j: (0, i, j, 0))
        weights = [wl["w_uv_pad"]]
    else:
        mla_spec = row_spec(omla.shape[2])
        weights = []
    weights += [wl["w_conv_out"], wl["w_mla_out"], wl["w_sb_out"], wl["w_out"]]
    return pl.pallas_call(
        functools.partial(_merge_kernel, absorbed=absorbed),
        grid=(s // ts, r // tr),
        in_specs=[row_spec(a.shape[2]), mla_spec, row_spec(osb.shape[2]), row_spec(gate.shape[2]), row_spec(d),
                  pl.BlockSpec((ts, 6, d), lambda i, j: (i, 0, 0))] + [_const_spec(w.shape) for w in weights],
        out_specs=row_spec(d),
        out_shape=jax.ShapeDtypeStruct((s, r, d), F32),
        compiler_params=_params("arbitrary", "arbitrary"),
        name="merge_sample" if absorbed else "merge_prompt",
    )(a, omla, osb, gate, x, mod, *weights)


def _ffn_kernel(*refs, final):
    (x_ref, mod_ref, ctx_ref, gn_ref, wup_ref, wdw_ref, bdw_ref, wdown_ref) = refs[:8]
    if final:
        gf_ref, o_ref, st_ref, y_ref, carry_ref, win_ref, act_ref = refs[8:]
    else:
        o_ref, st_ref, carry_ref, win_ref, act_ref = refs[8:]
    ts, tr, d = x_ref.shape
    tm = ts * tr
    pad = SUBLANE
    ctx_rows = FFN_K - 1

    @pl.when(pl.program_id(1) == 0)
    def _():
        carry_ref[...] = ctx_ref[...]

    x = x_ref[...]
    mod = mod_ref[...]
    h = _rms(x, gn_ref[...]) * (1.0 + mod[:, 4:5, :]) + mod[:, 3:4, :]
    hb = h.reshape(tm, d).astype(BF)

    def conv_chunk(c0):
        up = _dot(hb, wup_ref[:, c0:c0 + FFN_CHUNK]).reshape(ts, tr, FFN_CHUNK)
        win_ref[:, pad:pad + tr, :] = up
        win_ref[:, pad - ctx_rows:pad, :] = carry_ref[:, :, c0:c0 + FFN_CHUNK]
        y = bdw_ref[:, c0:c0 + FFN_CHUNK][None] + wdw_ref[2:3, c0:c0 + FFN_CHUNK][None] * up
        for k in range(ctx_rows):
            y = y + wdw_ref[k:k + 1, c0:c0 + FFN_CHUNK][None] * win_ref[:, pad - ctx_rows + k:pad - ctx_rows + k + tr, :]
        carry_ref[:, :, c0:c0 + FFN_CHUNK] = win_ref[:, pad + tr - ctx_rows:pad + tr, :]
        return y.reshape(tm, FFN_CHUNK)

    for i in range(D_FF // FFN_CHUNK):
        a = conv_chunk(i * FFN_CHUNK)
        g = conv_chunk(D_FF + i * FFN_CHUNK)
        act_ref[:, i * FFN_CHUNK:(i + 1) * FFN_CHUNK] = (a * jax.nn.sigmoid(a) * g).astype(BF)
    st_ref[...] = carry_ref[...]
    out = x + mod[:, 5:6, :] * _dot(act_ref[...], wdown_ref[...]).reshape(ts, tr, d)
    o_ref[...] = out
    if final:
        y_ref[...] = _rms(out, gf_ref[...])


def _ffn(x, mod, ctx, wl, g_final, *, final, ts, tr):
    s, r, d = x.shape

    def row_spec():
        return pl.BlockSpec((ts, tr, d), lambda i, j: (i, j, 0))

    state_spec = pl.BlockSpec((ts, FFN_K - 1, 2 * D_FF), lambda i, j: (i, 0, 0))
    weights = [wl["g_norm_ffn"], wl["w_ffn_up"], wl["w_ffn_dw"], wl["b_ffn_dw"], wl["w_ffn_down"]]
    if final:
        weights.append(g_final)
    out_specs = [row_spec(), state_spec]
    out_shape = [jax.ShapeDtypeStruct((s, r, d), F32), jax.ShapeDtypeStruct((s, FFN_K - 1, 2 * D_FF), F32)]
    if final:
        out_specs.append(row_spec())
        out_shape.append(jax.ShapeDtypeStruct((s, r, d), F32))
    return pl.pallas_call(
        functools.partial(_ffn_kernel, final=final),
        grid=(s // ts, r // tr),
        in_specs=[row_spec(), pl.BlockSpec((ts, 6, d), lambda i, j: (i, 0, 0)), state_spec]
        + [_const_spec(w.shape) for w in weights],
        out_specs=out_specs,
        out_shape=out_shape,
        scratch_shapes=[pltpu.VMEM((ts, FFN_K - 1, 2 * D_FF), F32),
                        pltpu.VMEM((ts, tr + SUBLANE, FFN_CHUNK), F32),
                        pltpu.VMEM((ts * tr, D_FF), BF)],
        compiler_params=_params("arbitrary", "arbitrary"),
        name="ffn_final" if final else "ffn",
    )(x, mod, ctx, *weights)


def _pad_last(w, n):
    return jnp.pad(w, [(0, 0)] * (w.ndim - 1) + [(0, n - w.shape[-1])])


def _prep_weights(w_in, w_uq, w_uk, w_uv):
    depth, d, _ = w_in.shape
    cuts = np.cumsum(IN_SIZES)[:-1]
    ua, ub, cq, ckv, kpe, sbq, sbk, sbv, gate = jnp.split(w_in, cuts, axis=-1)
    half = MLA_ROPE // 2
    rot = lambda w: jnp.concatenate([-w[..., half:], w[..., :half]], axis=-1)
    rep = LANE // MLA_ROPE
    kpe_t = jnp.tile(kpe, (1, 1, rep))
    kper_t = jnp.tile(rot(kpe), (1, 1, rep))
    sbq4 = sbq.reshape(depth, d, SB_HEADS, SB_DIM)
    per_group = SB_HEADS // SB_KV_HEADS
    in_g0 = (jnp.arange(SB_HEADS) < per_group)[None, None, :, None]
    sbq_pad = jnp.concatenate([jnp.where(in_g0, sbq4, 0.0), jnp.where(in_g0, 0.0, sbq4)], axis=-1)
    sbq_pad = sbq_pad.reshape(depth, d, SB_HEADS * HEAD_PAD)
    w_main = jnp.concatenate([ua, ub, cq, ckv, kpe_t, kper_t, sbq_pad, sbk, sbv, gate], axis=-1).astype(BF)

    wq4 = w_uq.reshape(depth, Q_RANK, MLA_HEADS, MLA_NOPE + MLA_ROPE)
    nope, pe = wq4[..., :MLA_NOPE], wq4[..., MLA_NOPE:]
    w_q = _pad_last(wq4, HEAD_PAD).reshape(depth, Q_RANK, -1).astype(BF)
    w_qr = _pad_last(jnp.concatenate([jnp.zeros_like(nope), rot(pe)], axis=-1), HEAD_PAD)
    w_qr = w_qr.reshape(depth, Q_RANK, -1).astype(BF)
    w_qpe = _pad_last(pe, HEAD_PAD).reshape(depth, Q_RANK, -1).astype(BF)
    w_qper = _pad_last(rot(pe), HEAD_PAD).reshape(depth, Q_RANK, -1).astype(BF)
    w_k = _pad_last(w_uk, HEAD_PAD).reshape(depth, KV_RANK, -1).astype(BF)
    w_v = w_uv.reshape(depth, KV_RANK, -1).astype(BF)
    w_ukt = jnp.transpose(w_uk, (0, 2, 3, 1))
    w_ukt = jnp.pad(w_ukt, ((0, 0), (0, 0), (0, HEAD_PAD - MLA_NOPE), (0, 0))).astype(BF)
    eye = jnp.eye(MLA_HEADS, dtype=w_uv.dtype)
    w_uv_pad = jnp.einsum("lrhv,hg->lhrgv", w_uv, eye).reshape(depth, MLA_HEADS, KV_RANK, MLA_HEADS * MLA_V)
    return dict(w_main=w_main, w_q=w_q, w_qr=w_qr, w_qpe=w_qpe, w_qper=w_qper, w_k=w_k, w_v=w_v,
                w_ukt=w_ukt, w_uv_pad=w_uv_pad.astype(BF))


def _rope_tables(pos):
    half = MLA_ROPE // 2
    inv_freq = ROPE_THETA ** (-jnp.arange(half, dtype=F32) / half)
    ang = pos.astype(F32)[:, None] * inv_freq[None, :]
    cos, sin = jnp.cos(ang), jnp.sin(ang)
    n = pos.shape[0]
    cos2 = jnp.concatenate([cos, cos], axis=-1)
    sin2 = jnp.concatenate([sin, sin], axis=-1)
    ones, zeros = jnp.ones((n, MLA_NOPE), F32), jnp.zeros((n, MLA_NOPE), F32)
    rep = LANE // MLA_ROPE
    return jnp.stack([
        MLA_SCALE * _pad_last(jnp.concatenate([ones, cos2], axis=-1), LANE),
        MLA_SCALE * _pad_last(jnp.concatenate([zeros, sin2], axis=-1), LANE),
        jnp.tile(cos2, (1, rep)),
        jnp.tile(sin2, (1, rep)),
        MLA_SCALE * _pad_last(cos2, LANE),
        MLA_SCALE * _pad_last(sin2, LANE)])


def kernel(x_prompt, x_sample, c_prompt, c_sample, cache_mla_latent, cache_mla_krope, cache_sb_k, cache_sb_v, state_conv, state_ffn_conv, page_table, w_ada, b_ada, g_norm_mix, g_norm_ffn, w_in, b_gate, w_dw, b_dw, g_conv_ln, b_conv_ln, w_conv_out, g_q, w_uq, g_kv, w_uk, w_uv, w_mla_out, w_sb_out, w_out, w_ffn_up, w_ffn_dw, b_ffn_dw, w_ffn_down, g_final):
    depth = w_in.shape[0]
    b, t, d = x_prompt.shape
    s, r, _ = x_sample.shape
    page = cache_mla_latent.shape[2]
    past_len = page_table.shape[1] * page
    assert r == SUBLANE and t % ATT_BLOCK == 0 and t % SB_BLOCK == 0 and t % ROW_TILE == 0
    assert s % SEQS_PER_TILE == 0 and page == LANE

    prepped = _prep_weights(w_in, w_uq, w_uk, w_uv)
    row = lambda w: w.reshape(depth, 1, w.shape[-1])
    stacked = dict(
        prepped,
        g_norm_mix=row(g_norm_mix), g_norm_ffn=row(g_norm_ffn), g_q=row(g_q), g_kv=row(g_kv),
        b_gate=row(b_gate), w_dw=w_dw, b_dw=row(b_dw), g_conv_ln=row(g_conv_ln), b_conv_ln=row(b_conv_ln),
        w_conv_out=w_conv_out.astype(BF), w_mla_out=w_mla_out.astype(BF), w_sb_out=w_sb_out.astype(BF),
        w_out=w_out.astype(BF), w_ffn_up=w_ffn_up.astype(BF), w_ffn_dw=w_ffn_dw, b_ffn_dw=row(b_ffn_dw),
        w_ffn_down=w_ffn_down.astype(BF))
    g_fin = g_final.reshape(1, d)

    tab_p = _rope_tables(jnp.arange(t, dtype=jnp.int32))
    tab_s = _rope_tables(past_len + jnp.arange(r, dtype=jnp.int32))

    n_mod = -(-(b + s) // SUBLANE) * SUBLANE
    c_all = jnp.concatenate([c_prompt, c_sample, jnp.zeros((n_mod - b - s, d), F32)], axis=0)
    mods = _ada(c_all, w_ada, b_ada)

    feat_major = lambda c: jnp.transpose(c, (0, 1, 3, 4, 2)).reshape(c.shape[:2] + (-1, c.shape[2]))
    caches = (cache_mla_latent, jnp.swapaxes(cache_mla_krope, 2, 3), feat_major(cache_sb_k), feat_major(cache_sb_v))
    ffn_zero = jnp.zeros((b, FFN_K - 1, 2 * D_FF), F32)

    xp, xs = x_prompt, x_sample
    yp = ys = None
    st_p, st_s = [], []
    for l in range(depth):
        wl = {k: v[l] for k, v in stacked.items()}
        final = l == depth - 1
        mod_p = mods[l, :b].reshape(b, 6, d)
        mod_s = mods[l, b:b + s].reshape(s, 6, d)

        (u, q, lat, kpe, sbq, sbk, sbv, gate, k, v, sbkb, sbvb) = _inproj(
            xp, mod_p, tab_p, wl, prompt=True, ts=1, tr=ROW_TILE)
        a = _conv_prompt(u, wl, tm=ROW_TILE)
        omla = _mla_prompt(q, k, v, blk=ATT_BLOCK)
        osb = _sb_prompt(sbq, sbkb, sbvb, blk=SB_BLOCK)
        x1 = _merge(a, omla, osb, gate, xp, mod_p, wl, absorbed=False, ts=1, tr=ROW_TILE)
        res = _ffn(x1, mod_p, ffn_zero, wl, g_fin, final=final, ts=1, tr=ROW_TILE)
        xp, ffn_state = res[0], res[1]
        if final:
            yp = res[2]
        st_p.append((lat, kpe, sbk.reshape(b, t, SB_KV_HEADS, SB_DIM), sbv.reshape(b, t, SB_KV_HEADS, SB_DIM),
                     u[:, t - (CONV_K - 1):], ffn_state))

        (u, lat, kpe, sbq, sbk, sbv, gate, qlat, qpe) = _inproj(
            xs, mod_s, tab_s, wl, prompt=False, ts=SEQS_PER_TILE, tr=r)
        full = jnp.concatenate([state_conv[l], u], axis=1)
        a = _conv_sample(full, wl, ts=SEQS_PER_TILE, tr=r)
        olat, osb = _decode(page_table, qlat, qpe, sbq, lat, kpe, sbk, sbv, caches, layer=l)
        x1 = _merge(a, olat, osb, gate, xs, mod_s, wl, absorbed=True, ts=SEQS_PER_TILE, tr=r)
        res = _ffn(x1, mod_s, state_ffn_conv[l], wl, g_fin, final=final, ts=SEQS_PER_TILE, tr=r)
        xs, ffn_state = res[0], res[1]
        if final:
            ys = res[2]
        st_s.append((lat, kpe, sbk.reshape(s, r, SB_KV_HEADS, SB_DIM), sbv.reshape(s, r, SB_KV_HEADS, SB_DIM),
                     full[:, r:], ffn_state))

    stack = lambda sts, i: jnp.stack([st[i] for st in sts], axis=0)
    return (yp, ys,
            stack(st_p, 0), stack(st_p, 1), stack(st_p, 2), stack(st_p, 3), stack(st_p, 4), stack(st_p, 5),
            stack(st_s, 0), stack(st_s, 1), stack(st_s, 2), stack(st_s, 3), stack(st_s, 4), stack(st_s, 5))
```
